```python
import math
import jax, jax.numpy as jnp
from jax import lax
import numpy as np

D_MODEL = 1024
BATCH = 2
SEQ = 8192
DEPTH = 2

CHUNK = 64
Q_BLOCK = 128
ROPE_THETA = 10000.0
RMS_EPS = 1e-6
D_FF = 2816

DA_HEADS = 4
DA_HEAD_DIM = 64
DA_V_DIM = 2 * DA_HEAD_DIM
DA_WIDTH = DA_HEADS * DA_V_DIM

ML_HEADS = 4
ML_QK_DIM = 64
ML_V_DIM = 128
ML_WIDTH = ML_HEADS * ML_V_DIM
ML_CONV = 4
ML_QK_COLS = 2 * ML_HEADS * ML_QK_DIM

MIX_WIDTH = DA_WIDTH + ML_WIDTH

COLS = (DA_HEADS * 2 * DA_HEAD_DIM, DA_HEADS * 2 * DA_HEAD_DIM, DA_WIDTH,
        ML_QK_COLS, ML_WIDTH, ML_WIDTH, 2 * ML_HEADS)
SPLIT_IDX = tuple(int(s) for s in np.cumsum(COLS)[:-1])
IN_COLS = int(sum(COLS))

kernel_name = 'hybrid_diffattn_mlstm_macaron'


def _rmsnorm(x, g):
    xf = x.astype(jnp.float32)
    y = xf * lax.rsqrt(jnp.mean(xf * xf, axis=-1, keepdims=True) + RMS_EPS)
    return (y * g.astype(jnp.float32)).astype(x.dtype)


def _swiglu(h, w_in, w_out):
    gate, up = jnp.split(h @ w_in, 2, axis=-1)
    return (jax.nn.silu(gate) * up) @ w_out


def _rope(t, seq_len):
    d = t.shape[-1]
    inv_freq = ROPE_THETA ** (-jnp.arange(0, d, 2, dtype=jnp.float32) / d)
    ang = jnp.arange(seq_len, dtype=jnp.float32)[:, None] * inv_freq[None, :]
    cos, sin = jnp.cos(ang), jnp.sin(ang)
    t1, t2 = t[..., : d // 2], t[..., d // 2:]
    return jnp.concatenate([t1 * cos - t2 * sin, t2 * cos + t1 * sin], axis=-1)


def _diff_attention(q, k, v, lam_vecs, subln_g, lam_init):
    B, S, _ = q.shape
    q = q.reshape(B, S, DA_HEADS, 2, DA_HEAD_DIM).transpose(0, 2, 3, 1, 4)
    k = k.reshape(B, S, DA_HEADS, 2, DA_HEAD_DIM).transpose(0, 2, 3, 1, 4)
    v = v.reshape(B, S, DA_HEADS, DA_V_DIM).transpose(0, 2, 1, 3)
    q = _rope(q, S) * (DA_HEAD_DIM ** -0.5)
    k = _rope(k, S)
    lam = (jnp.exp(jnp.sum(lam_vecs[0] * lam_vecs[1]))
           - jnp.exp(jnp.sum(lam_vecs[2] * lam_vecs[3])) + lam_init)
    n_blk = S // Q_BLOCK
    q_blocks = q.reshape(B, DA_HEADS, 2, n_blk, Q_BLOCK, DA_HEAD_DIM).transpose(3, 0, 1, 2, 4, 5)
    key_chunk = jnp.arange(S) // CHUNK

    def block(args):
        q_blk, blk = args
        s = jnp.einsum('bhcqd,bhckd->bhcqk', q_blk, k)
        q_chunk = (blk * Q_BLOCK + jnp.arange(Q_BLOCK)) // CHUNK
        mask = key_chunk[None, :] <= q_chunk[:, None]
        p = jax.nn.softmax(jnp.where(mask, s, -jnp.inf), axis=-1)
        a = p[:, :, 0] - lam * p[:, :, 1]
        return jnp.einsum('bhqk,bhkd->bhqd', a, v)

    o = lax.map(block, (q_blocks, jnp.arange(n_blk)))
    o = o.transpose(1, 0, 3, 2, 4).reshape(B, S, DA_HEADS, DA_V_DIM)
    o = _rmsnorm(o, subln_g) * (1.0 - lam_init)
    return o.reshape(B, S, DA_WIDTH)


def _causal_conv(t, w, b):
    S = t.shape[1]
    tp = jnp.pad(t, ((0, 0), (ML_CONV - 1, 0), (0, 0)))
    y = b
    for j in range(ML_CONV):
        y = y + w[j] * tp[:, j:j + S]
    return y


def _mlstm(qk, v, o, gates, conv_w, conv_b, gate_b, norm_g):
    B, S, _ = v.shape
    NC = S // CHUNK
    qk = jax.nn.silu(_causal_conv(qk, conv_w, conv_b))
    q, k = jnp.split(qk, 2, axis=-1)

    def chunks(t, d):
        return t.reshape(B, NC, CHUNK, ML_HEADS, d).transpose(0, 3, 1, 2, 4)

    q = chunks(q, ML_QK_DIM) * (ML_QK_DIM ** -0.5)
    k = chunks(k, ML_QK_DIM)
    v = chunks(v, ML_V_DIM)
    g = (gates + gate_b).reshape(B, NC, CHUNK, 2, ML_HEADS).transpose(3, 0, 4, 1, 2)
    log_i = g[0]
    log_f = jax.nn.log_sigmoid(g[1])
    b = jnp.cumsum(log_f, axis=-1)
    b_last = b[..., -1]
    a = b_last[..., None] - b + log_i

    def step(carry, xs):
        C, n, m = carry
        k_c, v_c, a_c, bl = xs
        m_new = jnp.maximum(bl + m, a_c.max(-1))
        w = jnp.exp(a_c - m_new[..., None])
        decay = jnp.exp(bl + m - m_new)
        C_new = decay[..., None, None] * C + jnp.einsum('bhl,bhle,bhld->bhed', w, v_c, k_c)
        n_new = decay[..., None] * n + jnp.einsum('bhl,bhld->bhd', w, k_c)
        return (C_new, n_new, m_new), (C, n, m)

    init = (jnp.zeros((B, ML_HEADS, ML_V_DIM, ML_QK_DIM), jnp.float32),
            jnp.zeros((B, ML_HEADS, ML_QK_DIM), jnp.float32),
            jnp.zeros((B, ML_HEADS), jnp.float32))
    xs = (k.transpose(2, 0, 1, 3, 4), v.transpose(2, 0, 1, 3, 4),
          a.transpose(2, 0, 1, 3), b_last.transpose(2, 0, 1))
    _, (C_prev, n_prev, m_prev) = lax.scan(step, init, xs)
    C_prev = C_prev.transpose(1, 2, 0, 3, 4)
    n_prev = n_prev.transpose(1, 2, 0, 3)
    m_prev = m_prev.transpose(1, 2, 0)

    causal = jnp.tril(jnp.ones((CHUNK, CHUNK), dtype=bool))
    D = jnp.where(causal, b[..., :, None] - b[..., None, :] + log_i[..., None, :], -jnp.inf)
    m_inter = b + m_prev[..., None]
    m = jnp.maximum(m_inter, D.max(-1))
    w = jnp.exp(D - m[..., None]) * jnp.einsum('bhcjd,bhcsd->bhcjs', q, k)
    decay = jnp.exp(m_inter - m)
    num = (jnp.einsum('bhcjs,bhcse->bhcje', w, v)
           + decay[..., None] * jnp.einsum('bhcjd,bhced->bhcje', q, C_prev))
    den = w.sum(-1) + decay * jnp.einsum('bhcjd,bhcd->bhcj', q, n_prev)
    h = num / jnp.maximum(jnp.abs(den), jnp.exp(-m))[..., None]
    h = h.transpose(0, 2, 3, 1, 4).reshape(B, S, ML_HEADS, ML_V_DIM)
    h = _rmsnorm(h, norm_g).reshape(B, S, ML_WIDTH)
    return jax.nn.sigmoid(o) * h


def _mixer(h, w_in, w_out, lam_vecs, subln_g, lam_init, conv_w, conv_b, gate_b, ml_norm_g):
    proj = (h @ w_in).astype(jnp.float32)
    da_q, da_k, da_v, ml_qk, ml_v, ml_o, ml_g = jnp.split(proj, SPLIT_IDX, axis=-1)
    y_da = _diff_attention(da_q, da_k, da_v, lam_vecs, subln_g, lam_init)
    y_ml = _mlstm(ml_qk, ml_v, ml_o, ml_g, conv_w, conv_b, gate_b, ml_norm_g)
    y = jnp.concatenate([y_da, y_ml], axis=-1).astype(h.dtype)
    return y @ w_out


def setup_inputs(seed: int = 0) -> dict:
    key = jax.random.key(seed)
    ks = jax.random.split(key, 13)
    f32 = jnp.float32
    x = jax.random.normal(ks[0], (BATCH, SEQ, D_MODEL), f32)
    ffn_w_in = jax.random.normal(ks[1], (DEPTH, 2, D_MODEL, 2 * D_FF), f32) * D_MODEL ** -0.5
    ffn_w_out = jax.random.normal(ks[2], (DEPTH, 2, D_FF, D_MODEL), f32) * D_FF ** -0.5
    norm_gains = 1.0 + 0.05 * jax.random.normal(ks[3], (DEPTH, 6, D_MODEL), f32)
    mix_w_in = jax.random.normal(ks[4], (DEPTH, D_MODEL, IN_COLS), f32) * D_MODEL ** -0.5
    mix_w_out = jax.random.normal(ks[5], (DEPTH, MIX_WIDTH, D_MODEL), f32) * MIX_WIDTH ** -0.5
    da_lambda = 0.1 * jax.random.normal(ks[6], (DEPTH, 4, DA_HEAD_DIM), f32)
    da_subln_g = 1.0 + 0.05 * jax.random.normal(ks[7], (DEPTH, DA_V_DIM), f32)
    ml_conv_w = jax.random.normal(ks[8], (DEPTH, ML_CONV, ML_QK_COLS), f32) * ML_CONV ** -0.5
    ml_conv_b = 0.01 * jax.random.normal(ks[9], (DEPTH, ML_QK_COLS), f32)
    ml_gate_b = jnp.concatenate([
        0.1 * jax.random.normal(ks[10], (DEPTH, ML_HEADS), f32),
        jax.random.uniform(ks[11], (DEPTH, ML_HEADS), f32, minval=3.0, maxval=6.0)], axis=-1)
    ml_norm_g = 1.0 + 0.05 * jax.random.normal(ks[12], (DEPTH, ML_V_DIM), f32)
    return {'x': x, 'ffn_w_in': ffn_w_in, 'ffn_w_out': ffn_w_out, 'norm_gains': norm_gains,
            'mix_w_in': mix_w_in, 'mix_w_out': mix_w_out, 'da_lambda': da_lambda,
            'da_subln_g': da_subln_g, 'ml_conv_w': ml_conv_w, 'ml_conv_b': ml_conv_b,
            'ml_gate_b': ml_gate_b, 'ml_norm_g': ml_norm_g}


def reference(x, ffn_w_in, ffn_w_out, norm_gains, mix_w_in, mix_w_out, da_lambda,
              da_subln_g, ml_conv_w, ml_conv_b, ml_gate_b, ml_norm_g):
    for l in range(DEPTH):
        g = norm_gains[l]
        h = _swiglu(_rmsnorm(x, g[0]), ffn_w_in[l, 0], ffn_w_out[l, 0])
        x = x + 0.5 * _rmsnorm(h, g[1])
        lam_init = 0.8 - 0.6 * math.exp(-0.3 * l)
        h = _mixer(_rmsnorm(x, g[2]), mix_w_in[l], mix_w_out[l], da_lambda[l], da_subln_g[l],
                   lam_init, ml_conv_w[l], ml_conv_b[l], ml_gate_b[l], ml_norm_g[l])
        x = x + _rmsnorm(h, g[3])
        h = _swiglu(_rmsnorm(x, g[4]), ffn_w_in[l, 1], ffn_w_out[l, 1])
        x = x + 0.5 * _rmsnorm(h, g[5])
    return x
```

```python
import functools
import math

import jax
import jax.numpy as jnp
import numpy as np
from jax import lax
from jax.experimental import pallas as pl
from jax.experimental.pallas import tpu as pltpu

D_MODEL = 1024
DEPTH = 2
CHUNK = 64
ROPE_THETA = 10000.0
RMS_EPS = 1e-6
D_FF = 2816
HEADS = 4
DA_HEAD_DIM = 64
HEAD_V = 128
GROUP_W = HEADS * HEAD_V
ML_QK = 64
ML_CONV = 4
N_MAIN = 6 * GROUP_W
N_GATES = 2 * HEADS

LANES = 128
SUBLANES = 8
VMEM_LIMIT = 52 * 1024 * 1024

FFN_ROWS = 512
FFN_COLS = 256
ATT_BLOCK = 256
ML_BLOCK = 256

BF16 = jnp.bfloat16
F32 = jnp.float32


def _dot(a, b):
    return jnp.dot(a, b, preferred_element_type=F32)


def _dot_nt(a, b):
    return lax.dot_general(a, b, (((1,), (1,)), ((), ())), preferred_element_type=F32)


def _dot_tn(a, b):
    return lax.dot_general(a, b, (((0,), (0,)), ((), ())), preferred_element_type=F32)


def _rms(x, g):
    return x * lax.rsqrt(jnp.mean(x * x, axis=-1, keepdims=True) + RMS_EPS) * g


def _params(*sem):
    return pltpu.CompilerParams(dimension_semantics=sem, vmem_limit_bytes=VMEM_LIMIT)


def _resident(shape):
    return pl.BlockSpec(shape, lambda *_: (0,) * len(shape), pipeline_mode=pl.Buffered(1))


def _ffn_body(x, g_pre, g_post, win_ref, wout_ref):
    xn = _rms(x, g_pre).astype(BF16)
    acc = jnp.zeros(x.shape, F32)
    for c in range(D_FF // FFN_COLS):
        lo = c * FFN_COLS
        gate = _dot(xn, win_ref[:, lo:lo + FFN_COLS])
        up = _dot(xn, win_ref[:, D_FF + lo:D_FF + lo + FFN_COLS])
        act = (gate * jax.nn.sigmoid(gate) * up).astype(BF16)
        acc = acc + _dot(act, wout_ref[lo:lo + FFN_COLS, :])
    return x + 0.5 * _rms(acc, g_post)


def _ffn_kernel(x_ref, gains_ref, win_ref, wout_ref, o_ref):
    g = gains_ref[...]
    o_ref[...] = _ffn_body(x_ref[...], g[0:1], g[1:2], win_ref, wout_ref)


def _mix_ffn_kernel(x_ref, yda_ref, yml_ref, wmix_ref, gains_ref, win_ref, wout_ref, o_ref):
    g = gains_ref[...]
    h = _dot(yda_ref[...], wmix_ref[0:GROUP_W, :]) + _dot(yml_ref[...], wmix_ref[GROUP_W:, :])
    x = x_ref[...] + _rms(h, g[0:1])
    o_ref[...] = _ffn_body(x, g[1:2], g[2:3], win_ref, wout_ref)


def _ffn_call(x, gains, w_in, w_out, mix=None):
    rows = x.shape[0]
    tm = FFN_ROWS
    row_spec = pl.BlockSpec((tm, D_MODEL), lambda i: (i, 0))
    half_spec = pl.BlockSpec((tm, GROUP_W), lambda i: (i, 0))
    weights = [_resident(gains.shape), _resident(w_in.shape), _resident(w_out.shape)]
    if mix is None:
        kern, ins, specs = _ffn_kernel, (x, gains, w_in, w_out), [row_spec] + weights
    else:
        y_da, y_ml, w_mix = mix
        kern = _mix_ffn_kernel
        ins = (x, y_da, y_ml, w_mix, gains, w_in, w_out)
        specs = [row_spec, half_spec, half_spec, _resident(w_mix.shape)] + weights
    return pl.pallas_call(
        kern,
        grid=(rows // tm,),
        in_specs=specs,
        out_specs=row_spec,
        out_shape=jax.ShapeDtypeStruct(x.shape, F32),
        compiler_params=_params("arbitrary"),
        name="ffn" if mix is None else "mix_ffn",
    )(*ins)


def _rope(t, cos, sin_signed, first_half):
    swapped = jnp.where(first_half, pltpu.roll(t, LANES - 32, axis=1), pltpu.roll(t, 32, axis=1))
    return t * cos + swapped * sin_signed


def _inproj_kernel(tiles_per_seq, x_ref, gain_ref, w_ref, wg_ref, cos_ref, sin_ref,
                   convw_ref, convb_ref,
                   q_ref, k_ref, v_ref, mq_ref, mk_ref, mv_ref, mo_ref, gates_ref,
                   conv_ref):
    tm = x_ref.shape[0]
    halo = SUBLANES
    xn = _rms(x_ref[...], gain_ref[...]).astype(BF16)

    cos = cos_ref[...]
    sin = sin_ref[...]
    lane = lax.broadcasted_iota(jnp.int32, (tm, LANES), 1)
    first_half = (lane % 64) < 32
    qk = _dot(xn, w_ref[:, 0:2 * GROUP_W])
    for h in range(HEADS):
        lo = h * LANES
        q_ref[:, lo:lo + LANES] = (
            _rope(qk[:, lo:lo + LANES], cos, sin, first_half) * (DA_HEAD_DIM ** -0.5)).astype(BF16)
        k_ref[:, lo:lo + LANES] = _rope(
            qk[:, GROUP_W + lo:GROUP_W + lo + LANES], cos, sin, first_half).astype(BF16)

    v_ref[...] = _dot(xn, w_ref[:, 2 * GROUP_W:3 * GROUP_W]).astype(BF16)

    @pl.when(pl.program_id(0) % tiles_per_seq == 0)
    def _():
        conv_ref[0:halo, :] = jnp.zeros((halo, GROUP_W), F32)

    conv_ref[halo:halo + tm, :] = _dot(xn, w_ref[:, 3 * GROUP_W:4 * GROUP_W])
    y = convb_ref[...] + convw_ref[ML_CONV - 1:ML_CONV, :] * conv_ref[halo:halo + tm, :]
    for j in range(ML_CONV - 1):
        back = ML_CONV - 1 - j
        y = y + convw_ref[j:j + 1, :] * conv_ref[halo - back:halo - back + tm, :]
    conv_ref[0:halo, :] = conv_ref[tm:tm + halo, :]
    y = y * jax.nn.sigmoid(y)
    mq_ref[...] = (y[:, 0:HEADS * ML_QK] * (ML_QK ** -0.5)).astype(BF16)
    mk_ref[...] = y[:, HEADS * ML_QK:].astype(BF16)

    mv_ref[...] = _dot(xn, w_ref[:, 4 * GROUP_W:5 * GROUP_W]).astype(BF16)
    mo_ref[...] = jax.nn.sigmoid(_dot(xn, w_ref[:, 5 * GROUP_W:6 * GROUP_W])).astype(BF16)
    gates_ref[...] = _dot(xn, wg_ref[...])[:, 0:N_GATES]


def _inproj_call(x, gain, w_main, w_gate, cos, sin, conv_w, conv_b, seq):
    rows = x.shape[0]
    tm = FFN_ROWS
    tiles_per_seq = seq // tm
    row = lambda w: pl.BlockSpec((tm, w), lambda i: (i, 0))
    rope_spec = pl.BlockSpec((tm, LANES), lambda i: (i % tiles_per_seq, 0))
    out_shapes = (
        jax.ShapeDtypeStruct((rows, GROUP_W), BF16),
        jax.ShapeDtypeStruct((rows, GROUP_W), BF16),
        jax.ShapeDtypeStruct((rows, GROUP_W), BF16),
        jax.ShapeDtypeStruct((rows, HEADS * ML_QK), BF16),
        jax.ShapeDtypeStruct((rows, HEADS * ML_QK), BF16),
        jax.ShapeDtypeStruct((rows, GROUP_W), BF16),
        jax.ShapeDtypeStruct((rows, GROUP_W), BF16),
        jax.ShapeDtypeStruct((rows, N_GATES), F32),
    )
    return pl.pallas_call(
        functools.partial(_inproj_kernel, tiles_per_seq),
        grid=(rows // tm,),
        in_specs=[row(D_MODEL), _resident(gain.shape), _resident(w_main.shape),
                  _resident(w_gate.shape), rope_spec, rope_spec,
                  _resident(conv_w.shape), _resident(conv_b.shape)],
        out_specs=tuple(row(s.shape[1]) for s in out_shapes),
        out_shape=out_shapes,
        scratch_shapes=[pltpu.VMEM((tm + 2 * SUBLANES, GROUP_W), F32)],
        compiler_params=_params("arbitrary"),
        name="mixer_inproj",
    )(x, gain, w_main, w_gate, cos, sin, conv_w, conv_b)


def _attn_kernel(lam_init, lam_ref, subg_ref, q_ref, k_ref, v_ref, o_ref, m_sc, l_sc, acc_sc):
    blk = ATT_BLOCK
    i = pl.program_id(2)
    q = q_ref[...]
    lane = lax.broadcasted_iota(jnp.int32, q.shape, 1)
    zero = jnp.zeros_like(q)
    q_halves = (jnp.where(lane < DA_HEAD_DIM, q, zero), jnp.where(lane >= DA_HEAD_DIM, q, zero))

    m_sc[...] = jnp.full(m_sc.shape, -jnp.inf, F32)
    l_sc[...] = jnp.zeros(l_sc.shape, F32)
    acc_sc[...] = jnp.zeros(acc_sc.shape, F32)

    def step(j, masked):
        start = pl.multiple_of(j * blk, blk)
        kb = k_ref[pl.ds(start, blk), :]
        vb = v_ref[pl.ds(start, blk), :]
        for c in range(2):
            s = _dot_nt(q_halves[c], kb)
            if masked:
                row = lax.broadcasted_iota(jnp.int32, s.shape, 0) // CHUNK
                col = lax.broadcasted_iota(jnp.int32, s.shape, 1) // CHUNK
                s = jnp.where(col <= row, s, -jnp.inf)
            m_old = m_sc[c]
            m_new = jnp.maximum(m_old, jnp.max(s, axis=-1, keepdims=True))
            alpha = jnp.exp(m_old - m_new)
            p = jnp.exp(s - m_new)
            l_sc[c] = alpha * l_sc[c] + jnp.sum(p, axis=-1, keepdims=True)
            acc_sc[c] = alpha * acc_sc[c] + _dot(p.astype(BF16), vb)
            m_sc[c] = m_new

    def body(j, carry):
        step(j, False)
        return carry

    lax.fori_loop(0, i, body, 0)
    step(i, True)

    lv = lam_ref[...]
    lam = (jnp.exp(jnp.sum(lv[0:1] * lv[1:2], axis=-1, keepdims=True))
           - jnp.exp(jnp.sum(lv[2:3] * lv[3:4], axis=-1, keepdims=True)) + lam_init)
    o = acc_sc[0] / l_sc[0] - lam * (acc_sc[1] / l_sc[1])
    o_ref[...] = (_rms(o, subg_ref[...]) * (1.0 - lam_init)).astype(BF16)


def _attn_call(q, k, v, lam_vecs, subln_g, lam_init, batch, seq):
    blk = ATT_BLOCK
    nq = seq // blk
    q_spec = pl.BlockSpec((blk, LANES), lambda b, h, i: (b * nq + i, h))
    kv_spec = pl.BlockSpec((seq, LANES), lambda b, h, i: (b, h))
    return pl.pallas_call(
        functools.partial(_attn_kernel, lam_init),
        grid=(batch, HEADS, nq),
        in_specs=[_resident(lam_vecs.shape), _resident(subln_g.shape), q_spec, kv_spec, kv_spec],
        out_specs=q_spec,
        out_shape=jax.ShapeDtypeStruct(q.shape, BF16),
        scratch_shapes=[pltpu.VMEM((2, blk, 1), F32), pltpu.VMEM((2, blk, 1), F32),
                        pltpu.VMEM((2, blk, LANES), F32)],
        compiler_params=_params("arbitrary", "arbitrary", "arbitrary"),
        name="diff_attention",
    )(lam_vecs, subln_g, q, k, v)


def _split3(x):
    hi = x.astype(BF16)
    r = x - hi.astype(F32)
    mid = r.astype(BF16)
    lo = (r - mid.astype(F32)).astype(BF16)
    return hi, mid, lo


def _gate_kernel(raw_ref, bias_ref, g_ref, b_ref):
    log_i = raw_ref[0] + bias_ref[0]
    f = raw_ref[1] + bias_ref[1]
    log_f = jnp.minimum(f, 0.0) - jnp.log1p(jnp.exp(-jnp.abs(f)))
    n = log_f.shape[1]
    src = lax.broadcasted_iota(jnp.int32, (n, n), 0)
    dst = lax.broadcasted_iota(jnp.int32, (n, n), 1)
    tri = jnp.where(src <= dst, 1.0, 0.0).astype(BF16)
    hi, mid, lo = _split3(log_f)
    b = _dot(hi, tri) + _dot(mid, tri) + _dot(lo, tri)
    g_ref[...] = log_i - b
    b_ref[...] = b


def _gate_call(raw, bias):
    shape = jax.ShapeDtypeStruct(raw.shape[1:], F32)
    return pl.pallas_call(
        _gate_kernel,
        out_shape=(shape, shape),
        compiler_params=pltpu.CompilerParams(vmem_limit_bytes=VMEM_LIMIT),
        name="mlstm_gates",
    )(raw, bias)


def _mlstm_kernel(q_ref, k_ref, v_ref, og_ref, grow_ref, gcol_ref, normg_ref, o_ref,
                  state_ref, m_ref):
    blk = q_ref.shape[0]

    @pl.when(pl.program_id(1) == 0)
    def _():
        state_ref[...] = jnp.zeros(state_ref.shape, F32)
        m_ref[...] = jnp.zeros(m_ref.shape, F32)

    row = lax.broadcasted_iota(jnp.int32, (blk, blk), 0)
    col = lax.broadcasted_iota(jnp.int32, (blk, blk), 1)
    causal = col <= row
    lane = lax.broadcasted_iota(jnp.int32, (blk, LANES), 1)
    ones_block = jnp.where(lane == 0, 1.0, 0.0).astype(BF16)
    normg = normg_ref[...]

    for h in range(HEADS):
        pair = (h // 2) * LANES
        in_head = (lane // ML_QK) == (h % 2)
        q2 = q_ref[:, pair:pair + LANES]
        k2 = k_ref[:, pair:pair + LANES]
        qh = jnp.where(in_head, q2, jnp.zeros_like(q2))
        v_ext = jnp.concatenate([v_ref[:, h * HEAD_V:(h + 1) * HEAD_V], ones_block], axis=1)

        g_row = grow_ref[h:h + 1, :]
        g_col = gcol_ref[:, h:h + 1]
        b_col = gcol_ref[:, HEADS + h:HEADS + h + 1]
        m_prev = m_ref[h:h + 1, 0:1]

        run_max = jnp.max(jnp.where(causal, g_row, -jnp.inf), axis=-1, keepdims=True)
        m_run = jnp.maximum(m_prev, run_max)
        s = _dot_nt(qh, k2)
        w = (jnp.where(causal, jnp.exp(g_row - m_run), 0.0) * s).astype(BF16)
        state = state_ref[h]
        nd = _dot(w, v_ext) + jnp.exp(m_prev - m_run) * _dot(q2, state.astype(BF16))
        num = nd[:, 0:HEAD_V]
        den = nd[:, HEAD_V:HEAD_V + 1]
        hid = num / jnp.maximum(jnp.abs(den), jnp.exp(-(b_col + m_run)))
        out = _rms(hid, normg) * og_ref[:, h * HEAD_V:(h + 1) * HEAD_V].astype(F32)
        o_ref[:, h * HEAD_V:(h + 1) * HEAD_V] = out.astype(BF16)

        m_last = m_run[blk - 1:blk, :]
        kw = jnp.where(in_head, k2.astype(F32) * jnp.exp(g_col - m_last), 0.0).astype(BF16)
        state_ref[h] = jnp.exp(m_prev - m_last) * state + _dot_tn(kw, v_ext)
        m_ref[h:h + 1, 0:1] = b_col[blk - 1:blk, :] + m_last


def _mlstm_call(q, k, v, og, g_row, g_col, norm_g, batch, seq):
    blk = ML_BLOCK
    nb = seq // blk
    row = lambda w: pl.BlockSpec((blk, w), lambda b, c: (b * nb + c, 0))
    return pl.pallas_call(
        _mlstm_kernel,
        grid=(batch, nb),
        in_specs=[row(HEADS * ML_QK), row(HEADS * ML_QK), row(GROUP_W), row(GROUP_W),
                  pl.BlockSpec((None, None, 2 * HEADS, blk), lambda b, c: (b, c, 0, 0)),
                  pl.BlockSpec((None, None, blk, 2 * HEADS), lambda b, c: (b, c, 0, 0)),
                  _resident(norm_g.shape)],
        out_specs=row(GROUP_W),
        out_shape=jax.ShapeDtypeStruct(v.shape, BF16),
        scratch_shapes=[pltpu.VMEM((HEADS, LANES, 2 * HEAD_V), F32),
                        pltpu.VMEM((SUBLANES, LANES), F32)],
        compiler_params=_params("arbitrary", "arbitrary"),
        name="mlstm",
    )(q, k, v, og, g_row, g_col, norm_g)


def _rope_tables(seq):
    half = DA_HEAD_DIM // 2
    inv_freq = ROPE_THETA ** (-jnp.arange(0, DA_HEAD_DIM, 2, dtype=F32) / DA_HEAD_DIM)
    ang = jnp.arange(seq, dtype=F32)[:, None] * inv_freq[None, :]
    cos, sin = jnp.cos(ang), jnp.sin(ang)
    reps = LANES // DA_HEAD_DIM
    cos_t = jnp.tile(jnp.concatenate([cos, cos], axis=1), (1, reps))
    sin_t = jnp.tile(jnp.concatenate([-sin, sin], axis=1), (1, reps))
    del half
    return cos_t, sin_t


def _mixer(x, gain, w_in, lam_vecs, subln_g, lam_init, conv_w, conv_b, gate_b, norm_g,
           rope, batch, seq):
    w_main = w_in[:, :N_MAIN].astype(BF16)
    w_gate = jnp.pad(w_in[:, N_MAIN:], ((0, 0), (0, LANES - N_GATES))).astype(BF16)
    q, k, v, mq, mk, mv, mo, gates = _inproj_call(
        x, gain, w_main, w_gate, rope[0], rope[1], conv_w, conv_b[None, :], seq)

    y_da = _attn_call(q, k, v, lam_vecs, subln_g[None, :], lam_init, batch, seq)

    blk = ML_BLOCK
    nb = seq // blk
    raw = gates.reshape(batch, nb, blk, 2, HEADS).transpose(3, 0, 4, 1, 2)
    raw = raw.reshape(2, batch * HEADS * nb, blk)
    bias = jnp.broadcast_to(gate_b.reshape(2, 1, HEADS, 1), (2, batch, HEADS, nb))
    bias = bias.reshape(2, batch * HEADS * nb, 1)
    g, b = _gate_call(raw, bias)
    gb = jnp.stack([g.reshape(batch, HEADS, nb, blk), b.reshape(batch, HEADS, nb, blk)], axis=1)
    gb = gb.reshape(batch, 2 * HEADS, nb, blk)
    g_row = gb.transpose(0, 2, 1, 3)
    g_col = gb.transpose(0, 2, 3, 1)
    y_ml = _mlstm_call(mq, mk, mv, mo, g_row, g_col, norm_g[None, :], batch, seq)
    return y_da, y_ml


def kernel(x, ffn_w_in, ffn_w_out, norm_gains, mix_w_in, mix_w_out, da_lambda, da_subln_g,
           ml_conv_w, ml_conv_b, ml_gate_b, ml_norm_g):
    batch, seq, d = x.shape
    assert d == D_MODEL and seq % FFN_ROWS == 0 and seq % ATT_BLOCK == 0 and seq % ML_BLOCK == 0
    rope = _rope_tables(seq)
    h = x.reshape(batch * seq, d)
    for l in range(DEPTH):
        g = norm_gains[l]
        w_in = ffn_w_in[l].astype(BF16)
        w_out = ffn_w_out[l].astype(BF16)
        h = _ffn_call(h, g[0:2], w_in[0], w_out[0])
        lam_init = 0.8 - 0.6 * math.exp(-0.3 * l)
        y_da, y_ml = _mixer(h, g[2:3], mix_w_in[l], da_lambda[l], da_subln_g[l], lam_init,
                            ml_conv_w[l], ml_conv_b[l], ml_gate_b[l], ml_norm_g[l],
                            rope, batch, seq)
        h = _ffn_call(h, g[3:6], w_in[1], w_out[1],
                      mix=(y_da, y_ml, mix_w_out[l].astype(BF16)))
    return h.reshape(batch, seq, d)
```

```python
import functools
import math

import jax
import jax.numpy as jnp
import numpy as np
from jax import lax
from jax.experimental import pallas as pl
from jax.experimental.pallas import tpu as pltpu

D_MODEL = 1024
DEPTH = 2
CHUNK = 64
ROPE_THETA = 10000.0
RMS_EPS = 1e-6
D_FF = 2816
HEADS = 4
DA_HEAD_DIM = 64
HEAD_V = 128
GROUP_W = HEADS * HEAD_V
ML_QK = 64
ML_CONV = 4
N_MAIN = 6 * GROUP_W
N_GATES = 2 * HEADS
Q_SCALE = DA_HEAD_DIM ** -0.5 * math.log2(math.e)

LANES = 128
SUBLANES = 8
VMEM_LIMIT = 52 * 1024 * 1024

FFN_ROWS = 512
FFN_COLS = 256
ATT_BLOCK = 512
ML_BLOCK = 256

BF16 = jnp.bfloat16
F32 = jnp.float32


def _dot(a, b):
    return jnp.dot(a, b, preferred_element_type=F32)


def _dot_nt(a, b):
    return lax.dot_general(a, b, (((1,), (1,)), ((), ())), preferred_element_type=F32)


def _dot_tn(a, b):
    return lax.dot_general(a, b, (((0,), (0,)), ((), ())), preferred_element_type=F32)


def _rms(x, g):
    return x * lax.rsqrt(jnp.mean(x * x, axis=-1, keepdims=True) + RMS_EPS) * g


def _params(*sem):
    return pltpu.CompilerParams(dimension_semantics=sem, vmem_limit_bytes=VMEM_LIMIT)


def _resident(shape):
    return pl.BlockSpec(shape, lambda *_: (0,) * len(shape), pipeline_mode=pl.Buffered(1))


def _ffn_body(x, g_pre, g_post, win_ref, wout_ref):
    xn = _rms(x, g_pre).astype(BF16)
    acc = jnp.zeros(x.shape, F32)
    for c in range(D_FF // FFN_COLS):
        lo = c * FFN_COLS
        gate = _dot(xn, win_ref[:, lo:lo + FFN_COLS])
        up = _dot(xn, win_ref[:, D_FF + lo:D_FF + lo + FFN_COLS])
        act = (gate * jax.nn.sigmoid(gate) * up).astype(BF16)
        acc = acc + _dot(act, wout_ref[lo:lo + FFN_COLS, :])
    return x + 0.5 * _rms(acc, g_post)


def _ffn_kernel(x_ref, gains_ref, win_ref, wout_ref, o_ref):
    g = gains_ref[...]
    o_ref[...] = _ffn_body(x_ref[...], g[0:1], g[1:2], win_ref, wout_ref)


def _mix_ffn_kernel(x_ref, yda_ref, yml_ref, wmix_ref, gains_ref, win_ref, wout_ref, o_ref):
    g = gains_ref[...]
    h = _dot(yda_ref[...], wmix_ref[0:GROUP_W, :]) + _dot(yml_ref[...], wmix_ref[GROUP_W:, :])
    x = x_ref[...] + _rms(h, g[0:1])
    o_ref[...] = _ffn_body(x, g[1:2], g[2:3], win_ref, wout_ref)


def _ffn_call(x, gains, w_in, w_out, mix=None):
    rows = x.shape[0]
    tm = FFN_ROWS
    row_spec = pl.BlockSpec((tm, D_MODEL), lambda i: (i, 0))
    half_spec = pl.BlockSpec((tm, GROUP_W), lambda i: (i, 0))
    weights = [_resident(gains.shape), _resident(w_in.shape), _resident(w_out.shape)]
    if mix is None:
        kern, ins, specs = _ffn_kernel, (x, gains, w_in, w_out), [row_spec] + weights
    else:
        y_da, y_ml, w_mix = mix
        kern = _mix_ffn_kernel
        ins = (x, y_da, y_ml, w_mix, gains, w_in, w_out)
        specs = [row_spec, half_spec, half_spec, _resident(w_mix.shape)] + weights
    return pl.pallas_call(
        kern,
        grid=(rows // tm,),
        in_specs=specs,
        out_specs=row_spec,
        out_shape=jax.ShapeDtypeStruct(x.shape, F32),
        compiler_params=_params("arbitrary"),
        name="ffn" if mix is None else "mix_ffn",
    )(*ins)


def _rope(t, cos, sin_signed, first_half):
    swapped = jnp.where(first_half, pltpu.roll(t, LANES - 32, axis=1), pltpu.roll(t, 32, axis=1))
    return t * cos + swapped * sin_signed


def _rope_rows(t, cos, sin):
    half = DA_HEAD_DIM // 2
    a, b = t[0:half], t[half:]
    return jnp.concatenate([a * cos - b * sin, b * cos + a * sin], axis=0)


def _inproj_kernel(tiles_per_seq, x_ref, gain_ref, w_ref, wt_ref, wg_ref, cos_ref, sin_ref,
                   cost_ref, sint_ref, convw_ref, convb_ref,
                   qt_ref, k_ref, vt_ref, mq_ref, mk_ref, mv_ref, mo_ref, gates_ref,
                   conv_ref):
    tm = x_ref.shape[0]
    halo = SUBLANES
    xn = _rms(x_ref[...], gain_ref[...]).astype(BF16)

    qv_t = _dot_nt(wt_ref[...], xn)
    cos_t = cost_ref[...]
    sin_t = sint_ref[...]
    for r in range(GROUP_W // DA_HEAD_DIM):
        lo = r * DA_HEAD_DIM
        qt_ref[lo:lo + DA_HEAD_DIM, :] = (
            _rope_rows(qv_t[lo:lo + DA_HEAD_DIM], cos_t, sin_t) * Q_SCALE).astype(BF16)
    vt_ref[...] = qv_t[GROUP_W:, :].astype(BF16)

    cos = cos_ref[...]
    sin = sin_ref[...]
    lane = lax.broadcasted_iota(jnp.int32, (tm, LANES), 1)
    first_half = (lane % 64) < 32
    kk = _dot(xn, w_ref[:, 0:GROUP_W])
    for h in range(HEADS):
        lo = h * LANES
        k_ref[:, lo:lo + LANES] = _rope(kk[:, lo:lo + LANES], cos, sin, first_half).astype(BF16)

    @pl.when(pl.program_id(0) % tiles_per_seq == 0)
    def _():
        conv_ref[0:halo, :] = jnp.zeros((halo, GROUP_W), F32)

    conv_ref[halo:halo + tm, :] = _dot(xn, w_ref[:, GROUP_W:2 * GROUP_W])
    y = convb_ref[...] + convw_ref[ML_CONV - 1:ML_CONV, :] * conv_ref[halo:halo + tm, :]
    for j in range(ML_CONV - 1):
        back = ML_CONV - 1 - j
        y = y + convw_ref[j:j + 1, :] * conv_ref[halo - back:halo - back + tm, :]
    conv_ref[0:halo, :] = conv_ref[tm:tm + halo, :]
    y = y * jax.nn.sigmoid(y)
    mq_ref[...] = (y[:, 0:HEADS * ML_QK] * (ML_QK ** -0.5)).astype(BF16)
    mk_ref[...] = y[:, HEADS * ML_QK:].astype(BF16)

    mv_ref[...] = _dot(xn, w_ref[:, 2 * GROUP_W:3 * GROUP_W]).astype(BF16)
    mo_ref[...] = jax.nn.sigmoid(_dot(xn, w_ref[:, 3 * GROUP_W:4 * GROUP_W])).astype(BF16)
    gates_ref[...] = _dot(xn, wg_ref[...])[:, 0:N_GATES]


def _inproj_call(x, gain, w_rows, w_cols_t, w_gate, rope, conv_w, conv_b, batch, seq):
    rows = x.shape[0]
    tm = FFN_ROWS
    tiles_per_seq = seq // tm
    row = lambda w: pl.BlockSpec((tm, w), lambda i: (i, 0))
    rope_spec = pl.BlockSpec((tm, LANES), lambda i: (i % tiles_per_seq, 0))
    rope_t_spec = pl.BlockSpec((DA_HEAD_DIM // 2, tm), lambda i: (0, i % tiles_per_seq))
    t_spec = pl.BlockSpec((GROUP_W, tm), lambda i: (i // tiles_per_seq, i % tiles_per_seq))
    out_shapes = (
        jax.ShapeDtypeStruct((batch * GROUP_W, seq), BF16),
        jax.ShapeDtypeStruct((rows, GROUP_W), BF16),
        jax.ShapeDtypeStruct((batch * GROUP_W, seq), BF16),
        jax.ShapeDtypeStruct((rows, HEADS * ML_QK), BF16),
        jax.ShapeDtypeStruct((rows, HEADS * ML_QK), BF16),
        jax.ShapeDtypeStruct((rows, GROUP_W), BF16),
        jax.ShapeDtypeStruct((rows, GROUP_W), BF16),
        jax.ShapeDtypeStruct((rows, N_GATES), F32),
    )
    out_specs = (t_spec, row(GROUP_W), t_spec, row(HEADS * ML_QK), row(HEADS * ML_QK),
                 row(GROUP_W), row(GROUP_W), row(N_GATES))
    cos, sin, cos_t, sin_t = rope
    return pl.pallas_call(
        functools.partial(_inproj_kernel, tiles_per_seq),
        grid=(rows // tm,),
        in_specs=[row(D_MODEL), _resident(gain.shape), _resident(w_rows.shape),
                  _resident(w_cols_t.shape), _resident(w_gate.shape),
                  rope_spec, rope_spec, rope_t_spec, rope_t_spec,
                  _resident(conv_w.shape), _resident(conv_b.shape)],
        out_specs=out_specs,
        out_shape=out_shapes,
        scratch_shapes=[pltpu.VMEM((tm + 2 * SUBLANES, GROUP_W), F32)],
        compiler_params=_params("arbitrary"),
        name="mixer_inproj",
    )(x, gain, w_rows, w_cols_t, w_gate, cos, sin, cos_t, sin_t, conv_w, conv_b)


def _attn_kernel(lam_init, lam_ref, subg_ref, qt_ref, k_ref, vt_ref, o_ref,
                 m_sc, l_sc, acc_sc, s_sc, bmax_sc):
    blk = ATT_BLOCK
    i = pl.program_id(2)
    qt = qt_ref[...]
    feat = lax.broadcasted_iota(jnp.int32, qt.shape, 0)
    zero = jnp.zeros_like(qt)
    qt_halves = (jnp.where(feat < DA_HEAD_DIM, qt, zero), jnp.where(feat >= DA_HEAD_DIM, qt, zero))

    m_sc[...] = jnp.full(m_sc.shape, -jnp.inf, F32)
    l_sc[...] = jnp.zeros(l_sc.shape, F32)
    acc_sc[...] = jnp.zeros(acc_sc.shape, F32)

    def chunk_mask():
        key = lax.broadcasted_iota(jnp.int32, (blk, blk), 0) // CHUNK
        qry = lax.broadcasted_iota(jnp.int32, (blk, blk), 1) // CHUNK
        return key <= qry

    def scores(j, slot, mask):
        kb = k_ref[pl.ds(pl.multiple_of(j * blk, blk), blk), :]
        for c in range(2):
            s = _dot(kb, qt_halves[c])
            if mask is not None:
                s = jnp.where(mask, s, -jnp.inf)
            s_sc[slot, c] = s
            bmax_sc[slot, c] = jnp.max(s, axis=0, keepdims=True)

    def accumulate(j, slot):
        vb = vt_ref[:, pl.ds(pl.multiple_of(j * blk, blk), blk)]
        for c in range(2):
            m_old = m_sc[c]
            m_new = jnp.maximum(m_old, bmax_sc[slot, c])
            alpha = jnp.exp2(m_old - m_new)
            p = jnp.exp2(s_sc[slot, c] - m_new)
            l_sc[c] = alpha * l_sc[c] + jnp.sum(p, axis=0, keepdims=True)
            acc_sc[c] = alpha * acc_sc[c] + _dot(vb, p.astype(BF16))
            m_sc[c] = m_new

    scores(0, 0, chunk_mask() | (i > 0))

    def pair(p, carry):
        j = 2 * p
        scores(j + 1, 1, None)
        accumulate(j, 0)
        scores(j + 2, 0, None)
        accumulate(j + 1, 1)
        return carry

    lax.fori_loop(0, jnp.maximum(i - 1, 0) // 2, pair, 0)
    odd = i % 2 == 1

    @pl.when(odd)
    def _():
        scores(i, 1, chunk_mask())
        accumulate(i - 1, 0)
        accumulate(i, 1)

    @pl.when(jnp.logical_and(jnp.logical_not(odd), i > 0))
    def _():
        scores(i - 1, 1, None)
        accumulate(i - 2, 0)
        scores(i, 0, chunk_mask())
        accumulate(i - 1, 1)
        accumulate(i, 0)

    @pl.when(i == 0)
    def _():
        accumulate(0, 0)

    lv = lam_ref[...]
    lam = (jnp.exp(jnp.sum(lv[0:1] * lv[1:2], axis=-1, keepdims=True))
           - jnp.exp(jnp.sum(lv[2:3] * lv[3:4], axis=-1, keepdims=True)) + lam_init)
    o_t = acc_sc[0] / l_sc[0] - lam * (acc_sc[1] / l_sc[1])
    o_ref[...] = (_rms(o_t.T, subg_ref[...]) * (1.0 - lam_init)).astype(BF16)


def _attn_call(qt, k, vt, lam_vecs, subln_g, lam_init, batch, seq):
    tq = ATT_BLOCK
    nq = seq // tq
    qt_spec = pl.BlockSpec((LANES, tq), lambda b, h, i: (b * HEADS + h, i))
    k_spec = pl.BlockSpec((seq, LANES), lambda b, h, i: (b, h))
    vt_spec = pl.BlockSpec((LANES, seq), lambda b, h, i: (b * HEADS + h, 0))
    o_spec = pl.BlockSpec((tq, LANES), lambda b, h, i: (b * nq + i, h))
    return pl.pallas_call(
        functools.partial(_attn_kernel, lam_init),
        grid=(batch, HEADS, nq),
        in_specs=[_resident(lam_vecs.shape), _resident(subln_g.shape), qt_spec, k_spec, vt_spec],
        out_specs=o_spec,
        out_shape=jax.ShapeDtypeStruct(k.shape, BF16),
        scratch_shapes=[pltpu.VMEM((2, 1, tq), F32), pltpu.VMEM((2, 1, tq), F32),
                        pltpu.VMEM((2, LANES, tq), F32),
                        pltpu.VMEM((2, 2, tq, tq), F32), pltpu.VMEM((2, 2, 1, tq), F32)],
        compiler_params=_params("arbitrary", "arbitrary", "arbitrary"),
        name="diff_attention",
    )(lam_vecs, subln_g, qt, k, vt)


def _split3(x):
    hi = x.astype(BF16)
    r = x - hi.astype(F32)
    mid = r.astype(BF16)
    lo = (r - mid.astype(F32)).astype(BF16)
    return hi, mid, lo


def _gate_kernel(raw_ref, bias_ref, g_ref, b_ref):
    log_i = raw_ref[0] + bias_ref[0]
    f = raw_ref[1] + bias_ref[1]
    log_f = jnp.minimum(f, 0.0) - jnp.log1p(jnp.exp(-jnp.abs(f)))
    n = log_f.shape[1]
    src = lax.broadcasted_iota(jnp.int32, (n, n), 0)
    dst = lax.broadcasted_iota(jnp.int32, (n, n), 1)
    tri = jnp.where(src <= dst, 1.0, 0.0).astype(BF16)
    hi, mid, lo = _split3(log_f)
    b = _dot(hi, tri) + _dot(mid, tri) + _dot(lo, tri)
    g_ref[...] = log_i - b
    b_ref[...] = b


def _gate_call(raw, bias):
    shape = jax.ShapeDtypeStruct(raw.shape[1:], F32)
    return pl.pallas_call(
        _gate_kernel,
        out_shape=(shape, shape),
        compiler_params=pltpu.CompilerParams(vmem_limit_bytes=VMEM_LIMIT),
        name="mlstm_gates",
    )(raw, bias)


def _mlstm_kernel(q_ref, k_ref, v_ref, og_ref, grow_ref, gcol_ref, normg_ref, o_ref,
                  state_ref, m_ref):
    blk = q_ref.shape[0]

    @pl.when(pl.program_id(1) == 0)
    def _():
        state_ref[...] = jnp.zeros(state_ref.shape, F32)
        m_ref[...] = jnp.zeros(m_ref.shape, F32)

    row = lax.broadcasted_iota(jnp.int32, (blk, blk), 0)
    col = lax.broadcasted_iota(jnp.int32, (blk, blk), 1)
    causal = col <= row
    lane = lax.broadcasted_iota(jnp.int32, (blk, LANES), 1)
    ones_block = jnp.where(lane == 0, 1.0, 0.0).astype(BF16)
    normg = normg_ref[...]

    for h in range(HEADS):
        pair = (h // 2) * LANES
        in_head = (lane // ML_QK) == (h % 2)
        q2 = q_ref[:, pair:pair + LANES]
        k2 = k_ref[:, pair:pair + LANES]
        qh = jnp.where(in_head, q2, jnp.zeros_like(q2))
        v_ext = jnp.concatenate([v_ref[:, h * HEAD_V:(h + 1) * HEAD_V], ones_block], axis=1)

        g_row = grow_ref[h:h + 1, :]
        g_col = gcol_ref[:, h:h + 1]
        b_col = gcol_ref[:, HEADS + h:HEADS + h + 1]
        m_prev = m_ref[h:h + 1, 0:1]

        run_max = jnp.max(jnp.where(causal, g_row, -jnp.inf), axis=-1, keepdims=True)
        m_run = jnp.maximum(m_prev, run_max)
        s = _dot_nt(qh, k2)
        w = (jnp.where(causal, jnp.exp(g_row - m_run), 0.0) * s).astype(BF16)
        state = state_ref[h]
        nd = _dot(w, v_ext) + jnp.exp(m_prev - m_run) * _dot(q2, state.astype(BF16))
        num = nd[:, 0:HEAD_V]
        den = nd[:, HEAD_V:HEAD_V + 1]
        hid = num / jnp.maximum(jnp.abs(den), jnp.exp(-(b_col + m_run)))
        out = _rms(hid, normg) * og_ref[:, h * HEAD_V:(h + 1) * HEAD_V].astype(F32)
        o_ref[:, h * HEAD_V:(h + 1) * HEAD_V] = out.astype(BF16)

        m_last = m_run[blk - 1:blk, :]
        kw = jnp.where(in_head, k2.astype(F32) * jnp.exp(g_col - m_last), 0.0).astype(BF16)
        state_ref[h] = jnp.exp(m_prev - m_last) * state + _dot_tn(kw, v_ext)
        m_ref[h:h + 1, 0:1] = b_col[blk - 1:blk, :] + m_last


def _mlstm_call(q, k, v, og, g_row, g_col, norm_g, batch, seq):
    blk = ML_BLOCK
    nb = seq // blk
    row = lambda w: pl.BlockSpec((blk, w), lambda b, c: (b * nb + c, 0))
    return pl.pallas_call(
        _mlstm_kernel,
        grid=(batch, nb),
        in_specs=[row(HEADS * ML_QK), row(HEADS * ML_QK), row(GROUP_W), row(GROUP_W),
                  pl.BlockSpec((None, None, 2 * HEADS, blk), lambda b, c: (b, c, 0, 0)),
                  pl.BlockSpec((None, None, blk, 2 * HEADS), lambda b, c: (b, c, 0, 0)),
                  _resident(norm_g.shape)],
        out_specs=row(GROUP_W),
        out_shape=jax.ShapeDtypeStruct(v.shape, BF16),
        scratch_shapes=[pltpu.VMEM((HEADS, LANES, 2 * HEAD_V), F32),
                        pltpu.VMEM((SUBLANES, LANES), F32)],
        compiler_params=_params("arbitrary", "arbitrary"),
        name="mlstm",
    )(q, k, v, og, g_row, g_col, norm_g)


def _rope_tables(seq):
    inv_freq = ROPE_THETA ** (-jnp.arange(0, DA_HEAD_DIM, 2, dtype=F32) / DA_HEAD_DIM)
    ang = jnp.arange(seq, dtype=F32)[:, None] * inv_freq[None, :]
    cos, sin = jnp.cos(ang), jnp.sin(ang)
    reps = LANES // DA_HEAD_DIM
    cos_l = jnp.tile(jnp.concatenate([cos, cos], axis=1), (1, reps))
    sin_l = jnp.tile(jnp.concatenate([-sin, sin], axis=1), (1, reps))
    return cos_l, sin_l, cos.T, sin.T


def _mixer(x, gain, w_in, lam_vecs, subln_g, lam_init, conv_w, conv_b, gate_b, norm_g,
           rope, batch, seq):
    w_bf = w_in.astype(BF16)
    w_rows = jnp.concatenate([w_bf[:, GROUP_W:2 * GROUP_W], w_bf[:, 3 * GROUP_W:N_MAIN]], axis=1)
    w_cols_t = jnp.concatenate([w_bf[:, 0:GROUP_W], w_bf[:, 2 * GROUP_W:3 * GROUP_W]], axis=1).T
    w_gate = jnp.pad(w_bf[:, N_MAIN:], ((0, 0), (0, LANES - N_GATES)))
    qt, k, vt, mq, mk, mv, mo, gates = _inproj_call(
        x, gain, w_rows, w_cols_t, w_gate, rope, conv_w, conv_b[None, :], batch, seq)

    y_da = _attn_call(qt, k, vt, lam_vecs, subln_g[None, :], lam_init, batch, seq)

    blk = ML_BLOCK
    nb = seq // blk
    raw = gates.reshape(batch, nb, blk, 2, HEADS).transpose(3, 0, 4, 1, 2)
    raw = raw.reshape(2, batch * HEADS * nb, blk)
    bias = jnp.broadcast_to(gate_b.reshape(2, 1, HEADS, 1), (2, batch, HEADS, nb))
    bias = bias.reshape(2, batch * HEADS * nb, 1)
    g, b = _gate_call(raw, bias)
    gb = jnp.stack([g.reshape(batch, HEADS, nb, blk), b.reshape(batch, HEADS, nb, blk)], axis=1)
    gb = gb.reshape(batch, 2 * HEADS, nb, blk)
    g_row = gb.transpose(0, 2, 1, 3)
    g_col = gb.transpose(0, 2, 3, 1)
    y_ml = _mlstm_call(mq, mk, mv, mo, g_row, g_col, norm_g[None, :], batch, seq)
    return y_da, y_ml


def kernel(x, ffn_w_in, ffn_w_out, norm_gains, mix_w_in, mix_w_out, da_lambda, da_subln_g,
           ml_conv_w, ml_conv_b, ml_gate_b, ml_norm_g):
    batch, seq, d = x.shape
    assert d == D_MODEL and seq % FFN_ROWS == 0 and seq % ML_BLOCK == 0
    assert seq % ATT_BLOCK == 0
    rope = _rope_tables(seq)
    h = x.reshape(batch * seq, d)
    for l in range(DEPTH):
        g = norm_gains[l]
        w_in = ffn_w_in[l].astype(BF16)
        w_out = ffn_w_out[l].astype(BF16)
        h = _ffn_call(h, g[0:2], w_in[0], w_out[0])
        lam_init = 0.8 - 0.6 * math.exp(-0.3 * l)
        y_da, y_ml = _mixer(h, g[2:3], mix_w_in[l], da_lambda[l], da_subln_g[l], lam_init,
                            ml_conv_w[l], ml_conv_b[l], ml_gate_b[l], ml_norm_g[l],
                            rope, batch, seq)
        h = _ffn_call(h, g[3:6], w_in[1], w_out[1],
                      mix=(y_da, y_ml, mix_w_out[l].astype(BF16)))
    return h.reshape(batch, seq, d)
```

```python
import functools
import math

import jax
import jax.numpy as jnp
import numpy as np
from jax import lax
from jax.experimental import pallas as pl
from jax.experimental.pallas import tpu as pltpu

D_MODEL = 1024
DEPTH = 2
CHUNK = 64
ROPE_THETA = 10000.0
RMS_EPS = 1e-6
D_FF = 2816
HEADS = 4
DA_HEAD_DIM = 64
HEAD_V = 128
GROUP_W = HEADS * HEAD_V
ML_QK = 64
ML_CONV = 4
N_MAIN = 6 * GROUP_W
N_GATES = 2 * HEADS
Q_SCALE = DA_HEAD_DIM ** -0.5 * math.log2(math.e)

LANES = 128
SUBLANES = 8
VMEM_LIMIT = 52 * 1024 * 1024

FFN_ROWS = 512
FFN_COLS = 256
ATT_BLOCK = 512
ML_BLOCK = 256

BF16 = jnp.bfloat16
F32 = jnp.float32


def _dot(a, b):
    return jnp.dot(a, b, preferred_element_type=F32)


def _dot_nt(a, b):
    return lax.dot_general(a, b, (((1,), (1,)), ((), ())), preferred_element_type=F32)


def _dot_tn(a, b):
    return lax.dot_general(a, b, (((0,), (0,)), ((), ())), preferred_element_type=F32)


def _rms(x, g):
    return x * lax.rsqrt(jnp.mean(x * x, axis=-1, keepdims=True) + RMS_EPS) * g


def _params(*sem):
    return pltpu.CompilerParams(dimension_semantics=sem, vmem_limit_bytes=VMEM_LIMIT)


def _resident(arr, *lead):
    tail = arr.shape[len(lead):]
    index = tuple(lead) + (0,) * len(tail)
    return pl.BlockSpec((None,) * len(lead) + tail, lambda *_: index,
                        pipeline_mode=pl.Buffered(1))


def _ffn_body(x, g_pre, g_post, win_ref, wout_ref):
    xn = _rms(x, g_pre).astype(BF16)
    acc = jnp.zeros(x.shape, F32)
    for c in range(D_FF // FFN_COLS):
        lo = c * FFN_COLS
        gate = _dot(xn, win_ref[:, lo:lo + FFN_COLS])
        up = _dot(xn, win_ref[:, D_FF + lo:D_FF + lo + FFN_COLS])
        act = (gate * jax.nn.sigmoid(gate) * up).astype(BF16)
        acc = acc + _dot(act, wout_ref[lo:lo + FFN_COLS, :])
    return x + 0.5 * _rms(acc, g_post)


def _ffn_kernel(x_ref, gains_ref, win_ref, wout_ref, o_ref):
    o_ref[...] = _ffn_body(x_ref[...], gains_ref[0:1], gains_ref[1:2], win_ref, wout_ref)


def _mix_ffn_kernel(x_ref, yda_ref, yml_ref, wmix_ref, gains_ref, win_ref, wout_ref, o_ref):
    h = _dot(yda_ref[...], wmix_ref[0:GROUP_W, :]) + _dot(yml_ref[...], wmix_ref[GROUP_W:, :])
    x = x_ref[...] + _rms(h, gains_ref[3:4])
    o_ref[...] = _ffn_body(x, gains_ref[4:5], gains_ref[5:6], win_ref, wout_ref)


def _ffn_call(x, gains, w_in, w_out, layer, mix=None):
    rows = x.shape[0]
    tm = FFN_ROWS
    which = 0 if mix is None else 1
    row_spec = pl.BlockSpec((tm, D_MODEL), lambda i: (i, 0))
    half_spec = pl.BlockSpec((tm, GROUP_W), lambda i: (i, 0))
    weights = [_resident(gains, layer), _resident(w_in, layer, which),
               _resident(w_out, layer, which)]
    if mix is None:
        kern, ins, specs = _ffn_kernel, (x, gains, w_in, w_out), [row_spec] + weights
    else:
        y_da, y_ml, w_mix = mix
        kern = _mix_ffn_kernel
        ins = (x, y_da, y_ml, w_mix, gains, w_in, w_out)
        specs = [row_spec, half_spec, half_spec, _resident(w_mix, layer)] + weights
    return pl.pallas_call(
        kern,
        grid=(rows // tm,),
        in_specs=specs,
        out_specs=row_spec,
        out_shape=jax.ShapeDtypeStruct(x.shape, F32),
        compiler_params=_params("arbitrary"),
        name="ffn" if mix is None else "mix_ffn",
    )(*ins)


def _rope(t, cos, sin_signed, first_half):
    swapped = jnp.where(first_half, pltpu.roll(t, LANES - 32, axis=1), pltpu.roll(t, 32, axis=1))
    return t * cos + swapped * sin_signed


def _rope_rows(t, cos, sin):
    half = DA_HEAD_DIM // 2
    a, b = t[0:half], t[half:]
    return jnp.concatenate([a * cos - b * sin, b * cos + a * sin], axis=0)


def _inproj_kernel(tiles_per_seq, x_ref, gain_ref, w_ref, wt_ref, wg_ref, cos_ref, sin_ref,
                   cost_ref, sint_ref, convw_ref, convb_ref,
                   qt_ref, k_ref, vt_ref, mq_ref, mk_ref, mv_ref, mo_ref, gates_ref,
                   conv_ref):
    tm = x_ref.shape[0]
    halo = SUBLANES
    xn = _rms(x_ref[...], gain_ref[2:3]).astype(BF16)

    qv_t = _dot_nt(wt_ref[...], xn)
    cos_t = cost_ref[...]
    sin_t = sint_ref[...]
    for r in range(GROUP_W // DA_HEAD_DIM):
        lo = r * DA_HEAD_DIM
        qt_ref[lo:lo + DA_HEAD_DIM, :] = (
            _rope_rows(qv_t[lo:lo + DA_HEAD_DIM], cos_t, sin_t) * Q_SCALE).astype(BF16)
    vt_ref[...] = qv_t[GROUP_W:2 * GROUP_W, :].astype(BF16)
    mv_ref[...] = qv_t[2 * GROUP_W:, :].astype(BF16)

    cos = cos_ref[...]
    sin = sin_ref[...]
    lane = lax.broadcasted_iota(jnp.int32, (tm, LANES), 1)
    first_half = (lane % 64) < 32
    kk = _dot(xn, w_ref[:, 0:GROUP_W])
    for h in range(HEADS):
        lo = h * LANES
        k_ref[:, lo:lo + LANES] = _rope(kk[:, lo:lo + LANES], cos, sin, first_half).astype(BF16)

    @pl.when(pl.program_id(0) % tiles_per_seq == 0)
    def _():
        conv_ref[0:halo, :] = jnp.zeros((halo, GROUP_W), F32)

    conv_ref[halo:halo + tm, :] = _dot(xn, w_ref[:, GROUP_W:2 * GROUP_W])
    y = convb_ref[...] + convw_ref[ML_CONV - 1:ML_CONV, :] * conv_ref[halo:halo + tm, :]
    for j in range(ML_CONV - 1):
        back = ML_CONV - 1 - j
        y = y + convw_ref[j:j + 1, :] * conv_ref[halo - back:halo - back + tm, :]
    conv_ref[0:halo, :] = conv_ref[tm:tm + halo, :]
    y = y * jax.nn.sigmoid(y)
    mq_ref[...] = (y[:, 0:HEADS * ML_QK] * (ML_QK ** -0.5)).astype(BF16)
    mk_ref[...] = y[:, HEADS * ML_QK:].astype(BF16)

    mo_ref[...] = jax.nn.sigmoid(_dot(xn, w_ref[:, 2 * GROUP_W:3 * GROUP_W])).astype(BF16)
    gates_ref[...] = _dot(xn, wg_ref[...])[:, 0:N_GATES]


def _inproj_call(x, gains, w_rows, w_cols_t, w_gate, rope, conv_w, conv_b, layer, batch, seq):
    rows = x.shape[0]
    tm = FFN_ROWS
    tiles_per_seq = seq // tm
    row = lambda w: pl.BlockSpec((tm, w), lambda i: (i, 0))
    rope_spec = pl.BlockSpec((tm, LANES), lambda i: (i % tiles_per_seq, 0))
    rope_t_spec = pl.BlockSpec((DA_HEAD_DIM // 2, tm), lambda i: (0, i % tiles_per_seq))
    t_spec = pl.BlockSpec((GROUP_W, tm), lambda i: (i // tiles_per_seq, i % tiles_per_seq))
    out_shapes = (
        jax.ShapeDtypeStruct((batch * GROUP_W, seq), BF16),
        jax.ShapeDtypeStruct((rows, GROUP_W), BF16),
        jax.ShapeDtypeStruct((batch * GROUP_W, seq), BF16),
        jax.ShapeDtypeStruct((rows, HEADS * ML_QK), BF16),
        jax.ShapeDtypeStruct((rows, HEADS * ML_QK), BF16),
        jax.ShapeDtypeStruct((batch * GROUP_W, seq), BF16),
        jax.ShapeDtypeStruct((rows, GROUP_W), BF16),
        jax.ShapeDtypeStruct((rows, N_GATES), F32),
    )
    out_specs = (t_spec, row(GROUP_W), t_spec, row(HEADS * ML_QK), row(HEADS * ML_QK),
                 t_spec, row(GROUP_W), row(N_GATES))
    cos, sin, cos_t, sin_t = rope
    return pl.pallas_call(
        functools.partial(_inproj_kernel, tiles_per_seq),
        grid=(rows // tm,),
        in_specs=[row(D_MODEL), _resident(gains, layer), _resident(w_rows, layer),
                  _resident(w_cols_t, layer), _resident(w_gate, layer),
                  rope_spec, rope_spec, rope_t_spec, rope_t_spec,
                  _resident(conv_w, layer), _resident(conv_b, layer)],
        out_specs=out_specs,
        out_shape=out_shapes,
        scratch_shapes=[pltpu.VMEM((tm + 2 * SUBLANES, GROUP_W), F32)],
        compiler_params=_params("arbitrary"),
        name="mixer_inproj",
    )(x, gains, w_rows, w_cols_t, w_gate, cos, sin, cos_t, sin_t, conv_w, conv_b)


def _attn_kernel(lam_init, lam_ref, subg_ref, qt_ref, k_ref, vt_ref, o_ref,
                 m_sc, l_sc, acc_sc, s_sc, bmax_sc):
    blk = ATT_BLOCK
    i = pl.program_id(2)
    qt = qt_ref[...]
    feat = lax.broadcasted_iota(jnp.int32, qt.shape, 0)
    zero = jnp.zeros_like(qt)
    qt_halves = (jnp.where(feat < DA_HEAD_DIM, qt, zero), jnp.where(feat >= DA_HEAD_DIM, qt, zero))

    m_sc[...] = jnp.full(m_sc.shape, -jnp.inf, F32)
    l_sc[...] = jnp.zeros(l_sc.shape, F32)
    acc_sc[...] = jnp.zeros(acc_sc.shape, F32)

    def chunk_mask():
        key = lax.broadcasted_iota(jnp.int32, (blk, blk), 0) // CHUNK
        qry = lax.broadcasted_iota(jnp.int32, (blk, blk), 1) // CHUNK
        return key <= qry

    def scores(j, slot, mask):
        kb = k_ref[pl.ds(pl.multiple_of(j * blk, blk), blk), :]
        for c in range(2):
            s = _dot(kb, qt_halves[c])
            if mask is not None:
                s = jnp.where(mask, s, -jnp.inf)
            s_sc[slot, c] = s
            bmax_sc[slot, c] = jnp.max(s, axis=0, keepdims=True)

    def accumulate(j, slot):
        vb = vt_ref[:, pl.ds(pl.multiple_of(j * blk, blk), blk)]
        for c in range(2):
            m_old = m_sc[c]
            m_new = jnp.maximum(m_old, bmax_sc[slot, c])
            alpha = jnp.exp2(m_old - m_new)
            p = jnp.exp2(s_sc[slot, c] - m_new)
            l_sc[c] = alpha * l_sc[c] + jnp.sum(p, axis=0, keepdims=True)
            acc_sc[c] = alpha * acc_sc[c] + _dot(vb, p.astype(BF16))
            m_sc[c] = m_new

    scores(i, 0, chunk_mask())

    def pair(p, carry):
        j = 2 * p
        scores(j, 1, None)
        accumulate(jnp.where(p == 0, i, j - 1), 0)
        scores(j + 1, 0, None)
        accumulate(j, 1)
        return carry

    lax.fori_loop(0, i // 2, pair, 0)

    @pl.when(i % 2 == 1)
    def _():
        scores(i - 1, 1, None)
        accumulate(jnp.where(i == 1, i, i - 2), 0)
        accumulate(i - 1, 1)

    @pl.when(i % 2 == 0)
    def _():
        accumulate(jnp.where(i == 0, i, i - 1), 0)

    lv = lam_ref[...]
    lam = (jnp.exp(jnp.sum(lv[0:1] * lv[1:2], axis=-1, keepdims=True))
           - jnp.exp(jnp.sum(lv[2:3] * lv[3:4], axis=-1, keepdims=True)) + lam_init)
    o_t = acc_sc[0] / l_sc[0] - lam * (acc_sc[1] / l_sc[1])
    o_ref[...] = (_rms(o_t.T, subg_ref[...]) * (1.0 - lam_init)).astype(BF16)


def _attn_call(qt, k, vt, lam_vecs, subln_g, layer, lam_init, batch, seq):
    tq = ATT_BLOCK
    nq = seq // tq
    qt_spec = pl.BlockSpec((LANES, tq), lambda b, h, i: (b * HEADS + h, i))
    k_spec = pl.BlockSpec((seq, LANES), lambda b, h, i: (b, h))
    vt_spec = pl.BlockSpec((LANES, seq), lambda b, h, i: (b * HEADS + h, 0))
    o_spec = pl.BlockSpec((tq, LANES), lambda b, h, i: (b * nq + i, h))
    return pl.pallas_call(
        functools.partial(_attn_kernel, lam_init),
        grid=(batch, HEADS, nq),
        in_specs=[_resident(lam_vecs, layer), _resident(subln_g, layer), qt_spec, k_spec, vt_spec],
        out_specs=o_spec,
        out_shape=jax.ShapeDtypeStruct(k.shape, BF16),
        scratch_shapes=[pltpu.VMEM((2, 1, tq), F32), pltpu.VMEM((2, 1, tq), F32),
                        pltpu.VMEM((2, LANES, tq), F32),
                        pltpu.VMEM((2, 2, tq, tq), F32), pltpu.VMEM((2, 2, 1, tq), F32)],
        compiler_params=_params("arbitrary", "arbitrary", "arbitrary"),
        name="diff_attention",
    )(lam_vecs, subln_g, qt, k, vt)


def _split3(x):
    hi = x.astype(BF16)
    r = x - hi.astype(F32)
    mid = r.astype(BF16)
    lo = (r - mid.astype(F32)).astype(BF16)
    return hi, mid, lo


def _gate_kernel(raw_ref, bias_ref, g_ref, b_ref):
    log_i = raw_ref[0] + bias_ref[0]
    f = raw_ref[1] + bias_ref[1]
    log_f = jnp.minimum(f, 0.0) - jnp.log1p(jnp.exp(-jnp.abs(f)))
    n = log_f.shape[1]
    src = lax.broadcasted_iota(jnp.int32, (n, n), 0)
    dst = lax.broadcasted_iota(jnp.int32, (n, n), 1)
    tri = jnp.where(src <= dst, 1.0, 0.0).astype(BF16)
    hi, mid, lo = _split3(log_f)
    b = _dot(hi, tri) + _dot(mid, tri) + _dot(lo, tri)
    g_ref[...] = log_i - b
    b_ref[...] = b


def _gate_call(raw, bias):
    shape = jax.ShapeDtypeStruct(raw.shape[1:], F32)
    return pl.pallas_call(
        _gate_kernel,
        out_shape=(shape, shape),
        compiler_params=pltpu.CompilerParams(vmem_limit_bytes=VMEM_LIMIT),
        name="mlstm_gates",
    )(raw, bias)


def _mlstm_kernel(q_ref, k_ref, vt_ref, og_ref, grow_ref, gcol_ref, normg_ref, o_ref,
                  state_ref, m_ref):
    blk = q_ref.shape[0]

    @pl.when(pl.program_id(1) == 0)
    def _():
        state_ref[...] = jnp.zeros(state_ref.shape, F32)
        m_ref[...] = jnp.zeros(m_ref.shape, F32)

    src = lax.broadcasted_iota(jnp.int32, (blk, blk), 0)
    dst = lax.broadcasted_iota(jnp.int32, (blk, blk), 1)
    causal = src <= dst
    lane = lax.broadcasted_iota(jnp.int32, (blk, LANES), 1)
    ones_rows = jnp.ones((HEAD_V, blk), BF16)
    normg = normg_ref[...]

    for h in range(HEADS):
        pair = (h // 2) * LANES
        in_head = (lane // ML_QK) == (h % 2)
        q2 = q_ref[:, pair:pair + LANES]
        k2 = k_ref[:, pair:pair + LANES]
        kh = jnp.where(in_head, k2, jnp.zeros_like(k2))
        v_ext = jnp.concatenate([vt_ref[h * HEAD_V:(h + 1) * HEAD_V, :], ones_rows], axis=0)

        b_row = grow_ref[HEADS + h:HEADS + h + 1, :]
        g_src = jnp.broadcast_to(gcol_ref[:, h:h + 1], (blk, blk))
        m_prev = m_ref[h:h + 1, 0:1]

        run_max = jnp.max(jnp.where(causal, g_src, -jnp.inf), axis=0, keepdims=True)
        m_run = jnp.maximum(m_prev, run_max)
        s_t = _dot_nt(kh, q2)
        w_t = (jnp.where(causal, jnp.exp(g_src - m_run), 0.0) * s_t).astype(BF16)
        state = state_ref[h]
        nd = _dot(v_ext, w_t) + jnp.exp(m_prev - m_run) * _dot_nt(state.astype(BF16), q2)
        num = nd[0:HEAD_V]
        den = nd[HEAD_V:]
        hid = num / jnp.maximum(jnp.abs(den), jnp.exp(-(b_row + m_run)))
        hid = hid * lax.rsqrt(jnp.mean(hid * hid, axis=0, keepdims=True) + RMS_EPS)
        out = hid.T * normg * og_ref[:, h * HEAD_V:(h + 1) * HEAD_V].astype(F32)
        o_ref[:, h * HEAD_V:(h + 1) * HEAD_V] = out.astype(BF16)

        m_last = m_run[:, blk - 1:blk]
        kw = jnp.where(in_head, k2.astype(F32) * jnp.exp(g_src[:, 0:LANES] - m_last), 0.0)
        state_ref[h] = jnp.exp(m_prev - m_last) * state + _dot(v_ext, kw.astype(BF16))
        m_ref[h:h + 1, 0:1] = b_row[:, blk - 1:blk] + m_last


def _mlstm_call(q, k, vt, og, g_row, g_col, norm_g, layer, batch, seq):
    blk = ML_BLOCK
    nb = seq // blk
    row = lambda w: pl.BlockSpec((blk, w), lambda b, c: (b * nb + c, 0))
    return pl.pallas_call(
        _mlstm_kernel,
        grid=(batch, nb),
        in_specs=[row(HEADS * ML_QK), row(HEADS * ML_QK),
                  pl.BlockSpec((GROUP_W, blk), lambda b, c: (b, c)), row(GROUP_W),
                  pl.BlockSpec((None, None, 2 * HEADS, blk), lambda b, c: (b, c, 0, 0)),
                  pl.BlockSpec((None, None, blk, 2 * HEADS), lambda b, c: (b, c, 0, 0)),
                  _resident(norm_g, layer)],
        out_specs=row(GROUP_W),
        out_shape=jax.ShapeDtypeStruct(og.shape, BF16),
        scratch_shapes=[pltpu.VMEM((HEADS, 2 * HEAD_V, LANES), F32),
                        pltpu.VMEM((SUBLANES, LANES), F32)],
        compiler_params=_params("arbitrary", "arbitrary"),
        name="mlstm",
    )(q, k, vt, og, g_row, g_col, norm_g)


def _rope_tables(seq):
    inv_freq = ROPE_THETA ** (-jnp.arange(0, DA_HEAD_DIM, 2, dtype=F32) / DA_HEAD_DIM)
    ang = jnp.arange(seq, dtype=F32)[:, None] * inv_freq[None, :]
    cos, sin = jnp.cos(ang), jnp.sin(ang)
    reps = LANES // DA_HEAD_DIM
    cos_l = jnp.tile(jnp.concatenate([cos, cos], axis=1), (1, reps))
    sin_l = jnp.tile(jnp.concatenate([-sin, sin], axis=1), (1, reps))
    return cos_l, sin_l, cos.T, sin.T


def _mixer(x, p, layer, lam_init, rope, batch, seq):
    qt, k, vt, mq, mk, mv, mo, gates = _inproj_call(
        x, p["gains"], p["w_rows"], p["w_cols_t"], p["w_gate"], rope, p["conv_w"], p["conv_b"],
        layer, batch, seq)

    y_da = _attn_call(qt, k, vt, p["lam"], p["subln_g"], layer, lam_init, batch, seq)

    blk = ML_BLOCK
    nb = seq // blk
    raw = gates.reshape(batch, nb, blk, 2, HEADS).transpose(3, 0, 4, 1, 2)
    raw = raw.reshape(2, batch * HEADS * nb, blk)
    g, b = _gate_call(raw, p["gate_bias"][layer])
    gb = jnp.stack([g.reshape(batch, HEADS, nb, blk), b.reshape(batch, HEADS, nb, blk)], axis=1)
    gb = gb.reshape(batch, 2 * HEADS, nb, blk)
    g_row = gb.transpose(0, 2, 1, 3)
    g_col = gb.transpose(0, 2, 3, 1)
    y_ml = _mlstm_call(mq, mk, mv, mo, g_row, g_col, p["norm_g"], layer, batch, seq)
    return y_da, y_ml


def _prepare(norm_gains, mix_w_in, mix_w_out, da_lambda, da_subln_g, ml_conv_w, ml_conv_b,
             ml_gate_b, ml_norm_g, batch, seq):
    w_bf = mix_w_in.astype(BF16)
    nb = seq // ML_BLOCK
    bias = jnp.broadcast_to(ml_gate_b.reshape(DEPTH, 2, 1, HEADS, 1), (DEPTH, 2, batch, HEADS, nb))
    return {
        "gains": norm_gains,
        "w_rows": jnp.concatenate([w_bf[:, :, GROUP_W:2 * GROUP_W],
                                   w_bf[:, :, 3 * GROUP_W:4 * GROUP_W],
                                   w_bf[:, :, 5 * GROUP_W:N_MAIN]], axis=2),
        "w_cols_t": jnp.concatenate([w_bf[:, :, 0:GROUP_W], w_bf[:, :, 2 * GROUP_W:3 * GROUP_W],
                                     w_bf[:, :, 4 * GROUP_W:5 * GROUP_W]],
                                    axis=2).transpose(0, 2, 1),
        "w_gate": jnp.pad(w_bf[:, :, N_MAIN:], ((0, 0), (0, 0), (0, LANES - N_GATES))),
        "w_mix_out": mix_w_out.astype(BF16),
        "lam": da_lambda,
        "subln_g": da_subln_g[:, None, :],
        "conv_w": ml_conv_w,
        "conv_b": ml_conv_b[:, None, :],
        "gate_bias": bias.reshape(DEPTH, 2, batch * HEADS * nb, 1),
        "norm_g": ml_norm_g[:, None, :],
    }


def kernel(x, ffn_w_in, ffn_w_out, norm_gains, mix_w_in, mix_w_out, da_lambda, da_subln_g,
           ml_conv_w, ml_conv_b, ml_gate_b, ml_norm_g):
    batch, seq, d = x.shape
    assert d == D_MODEL and seq % FFN_ROWS == 0 and seq % ML_BLOCK == 0
    assert seq % ATT_BLOCK == 0
    rope = _rope_tables(seq)
    p = _prepare(norm_gains, mix_w_in, mix_w_out, da_lambda, da_subln_g, ml_conv_w, ml_conv_b,
                 ml_gate_b, ml_norm_g, batch, seq)
    w_in = ffn_w_in.astype(BF16)
    w_out = ffn_w_out.astype(BF16)
    h = x.reshape(batch * seq, d)
    for l in range(DEPTH):
        h = _ffn_call(h, norm_gains, w_in, w_out, l)
        lam_init = 0.8 - 0.6 * math.exp(-0.3 * l)
        y_da, y_ml = _mixer(h, p, l, lam_init, rope, batch, seq)
        h = _ffn_call(h, norm_gains, w_in, w_out, l, mix=(y_da, y_ml, p["w_mix_out"]))
    return h.reshape(batch, seq, d)
```

```python
import functools
import math

import jax
import jax.numpy as jnp
import numpy as np
from jax import lax
from jax.experimental import pallas as pl
from jax.experimental.pallas import tpu as pltpu

D_MODEL = 1024
DEPTH = 2
CHUNK = 64
ROPE_THETA = 10000.0
RMS_EPS = 1e-6
D_FF = 2816
HEADS = 4
DA_HEAD_DIM = 64
HEAD_V = 128
GROUP_W = HEADS * HEAD_V
ML_QK = 64
ML_CONV = 4
N_MAIN = 6 * GROUP_W
N_GATES = 2 * HEADS
Q_SCALE = DA_HEAD_DIM ** -0.5 * math.log2(math.e)

LANES = 128
SUBLANES = 8
VMEM_LIMIT = 52 * 1024 * 1024

FFN_ROWS = 512
FFN_COLS = 256
ROW_SPLIT = 2
ATT_BLOCK = 512
ATT_L_LIMIT = 2.0 ** 64
ML_BLOCK = 256

BF16 = jnp.bfloat16
F32 = jnp.float32


def _dot(a, b):
    return jnp.dot(a, b, preferred_element_type=F32)


def _dot_nt(a, b):
    return lax.dot_general(a, b, (((1,), (1,)), ((), ())), preferred_element_type=F32)


def _dot_tn(a, b):
    return lax.dot_general(a, b, (((0,), (0,)), ((), ())), preferred_element_type=F32)


def _rms(x, g):
    return x * lax.rsqrt(jnp.mean(x * x, axis=-1, keepdims=True) + RMS_EPS) * g


def _params(*sem):
    return pltpu.CompilerParams(dimension_semantics=sem, vmem_limit_bytes=VMEM_LIMIT)


def _resident(arr, *lead):
    tail = arr.shape[len(lead):]
    index = tuple(lead) + (0,) * len(tail)
    return pl.BlockSpec((None,) * len(lead) + tail, lambda *_: index,
                        pipeline_mode=pl.Buffered(1))


def _ffn_body(x, g_pre, g_post, win_ref, wout_ref):
    xn = _rms(x, g_pre).astype(BF16)
    acc = jnp.zeros(x.shape, F32)
    for c in range(D_FF // FFN_COLS):
        lo = c * FFN_COLS
        gate = _dot(xn, win_ref[:, lo:lo + FFN_COLS])
        up = _dot(xn, win_ref[:, D_FF + lo:D_FF + lo + FFN_COLS])
        act = (gate * jax.nn.sigmoid(gate) * up).astype(BF16)
        acc = acc + _dot(act, wout_ref[lo:lo + FFN_COLS, :])
    return x + 0.5 * _rms(acc, g_post)


def _ffn_kernel(x_ref, gains_ref, win_ref, wout_ref, o_ref):
    o_ref[...] = _ffn_body(x_ref[...], gains_ref[0:1], gains_ref[1:2], win_ref, wout_ref)


def _mix_ffn_kernel(x_ref, yda_ref, yml_ref, wmix_ref, gains_ref, win_ref, wout_ref, o_ref):
    h = _dot(yda_ref[...], wmix_ref[0:GROUP_W, :]) + _dot(yml_ref[...], wmix_ref[GROUP_W:, :])
    x = x_ref[...] + _rms(h, gains_ref[3:4])
    o_ref[...] = _ffn_body(x, gains_ref[4:5], gains_ref[5:6], win_ref, wout_ref)


def _ffn_call(x, gains, w_in, w_out, layer, mix=None):
    rows = x.shape[0]
    tm = FFN_ROWS
    which = 0 if mix is None else 1
    row_spec = pl.BlockSpec((tm, D_MODEL), lambda i: (i, 0))
    half_spec = pl.BlockSpec((tm, GROUP_W), lambda i: (i, 0))
    weights = [_resident(gains, layer), _resident(w_in, layer, which),
               _resident(w_out, layer, which)]
    if mix is None:
        kern, ins, specs = _ffn_kernel, (x, gains, w_in, w_out), [row_spec] + weights
    else:
        y_da, y_ml, w_mix = mix
        kern = _mix_ffn_kernel
        ins = (x, y_da, y_ml, w_mix, gains, w_in, w_out)
        specs = [row_spec, half_spec, half_spec, _resident(w_mix, layer)] + weights
    return pl.pallas_call(
        kern,
        grid=(rows // tm,),
        in_specs=specs,
        out_specs=row_spec,
        out_shape=jax.ShapeDtypeStruct(x.shape, F32),
        compiler_params=_params("arbitrary"),
        name="ffn" if mix is None else "mix_ffn",
    )(*ins)


def _rope(t, cos, sin_signed, first_half):
    swapped = jnp.where(first_half, pltpu.roll(t, LANES - 32, axis=1), pltpu.roll(t, 32, axis=1))
    return t * cos + swapped * sin_signed


def _rope_rows(t, cos, sin):
    half = DA_HEAD_DIM // 2
    a, b = t[0:half], t[half:]
    return jnp.concatenate([a * cos - b * sin, b * cos + a * sin], axis=0)


def _inproj_kernel(tiles_per_seq, x_ref, gain_ref, w_ref, wt_ref, wg_ref, cos_ref, sin_ref,
                   cost_ref, sint_ref, convw_ref, convb_ref,
                   qt_ref, k_ref, vt_ref, mq_ref, mk_ref, mv_ref, mo_ref, gates_ref,
                   conv_ref):
    tm = x_ref.shape[0]
    halo = SUBLANES
    sub = tm // ROW_SPLIT

    @pl.when(pl.program_id(0) % tiles_per_seq == 0)
    def _():
        conv_ref[0:halo, :] = jnp.zeros((halo, GROUP_W), F32)

    lane = lax.broadcasted_iota(jnp.int32, (sub, LANES), 1)
    first_half = (lane % 64) < 32

    for part in range(ROW_SPLIT):
        r0 = part * sub
        rows = slice(r0, r0 + sub)
        xn = _rms(x_ref[rows, :], gain_ref[2:3]).astype(BF16)

        qv_t = _dot_nt(wt_ref[...], xn)
        cos_t = cost_ref[:, rows]
        sin_t = sint_ref[:, rows]
        for r in range(GROUP_W // DA_HEAD_DIM):
            lo = r * DA_HEAD_DIM
            qt_ref[lo:lo + DA_HEAD_DIM, rows] = (
                _rope_rows(qv_t[lo:lo + DA_HEAD_DIM], cos_t, sin_t) * Q_SCALE).astype(BF16)
        vt_ref[:, rows] = qv_t[GROUP_W:2 * GROUP_W, :].astype(BF16)
        mv_ref[:, rows] = qv_t[2 * GROUP_W:, :].astype(BF16)

        cos = cos_ref[rows, :]
        sin = sin_ref[rows, :]
        kk = _dot(xn, w_ref[:, 0:GROUP_W])
        for h in range(HEADS):
            lo = h * LANES
            k_ref[rows, lo:lo + LANES] = _rope(kk[:, lo:lo + LANES], cos, sin, first_half).astype(BF16)

        base = halo + r0
        conv_ref[base:base + sub, :] = _dot(xn, w_ref[:, GROUP_W:2 * GROUP_W])
        y = convb_ref[...] + convw_ref[ML_CONV - 1:ML_CONV, :] * conv_ref[base:base + sub, :]
        for j in range(ML_CONV - 1):
            back = ML_CONV - 1 - j
            y = y + convw_ref[j:j + 1, :] * conv_ref[base - back:base - back + sub, :]
        y = y * jax.nn.sigmoid(y)
        mq_ref[rows, :] = (y[:, 0:HEADS * ML_QK] * (ML_QK ** -0.5)).astype(BF16)
        mk_ref[rows, :] = y[:, HEADS * ML_QK:].astype(BF16)

        mo_ref[rows, :] = jax.nn.sigmoid(_dot(xn, w_ref[:, 2 * GROUP_W:3 * GROUP_W])).astype(BF16)
        gates_ref[rows, :] = _dot(xn, wg_ref[...])[:, 0:N_GATES]

    conv_ref[0:halo, :] = conv_ref[tm:tm + halo, :]


def _inproj_call(x, gains, w_rows, w_cols_t, w_gate, rope, conv_w, conv_b, layer, batch, seq):
    rows = x.shape[0]
    tm = FFN_ROWS
    tiles_per_seq = seq // tm
    row = lambda w: pl.BlockSpec((tm, w), lambda i: (i, 0))
    rope_spec = pl.BlockSpec((tm, LANES), lambda i: (i % tiles_per_seq, 0))
    rope_t_spec = pl.BlockSpec((DA_HEAD_DIM // 2, tm), lambda i: (0, i % tiles_per_seq))
    t_spec = pl.BlockSpec((GROUP_W, tm), lambda i: (i // tiles_per_seq, i % tiles_per_seq))
    out_shapes = (
        jax.ShapeDtypeStruct((batch * GROUP_W, seq), BF16),
        jax.ShapeDtypeStruct((rows, GROUP_W), BF16),
        jax.ShapeDtypeStruct((batch * GROUP_W, seq), BF16),
        jax.ShapeDtypeStruct((rows, HEADS * ML_QK), BF16),
        jax.ShapeDtypeStruct((rows, HEADS * ML_QK), BF16),
        jax.ShapeDtypeStruct((batch * GROUP_W, seq), BF16),
        jax.ShapeDtypeStruct((rows, GROUP_W), BF16),
        jax.ShapeDtypeStruct((rows, N_GATES), F32),
    )
    out_specs = (t_spec, row(GROUP_W), t_spec, row(HEADS * ML_QK), row(HEADS * ML_QK),
                 t_spec, row(GROUP_W), row(N_GATES))
    cos, sin, cos_t, sin_t = rope
    return pl.pallas_call(
        functools.partial(_inproj_kernel, tiles_per_seq),
        grid=(rows // tm,),
        in_specs=[row(D_MODEL), _resident(gains, layer), _resident(w_rows, layer),
                  _resident(w_cols_t, layer), _resident(w_gate, layer),
                  rope_spec, rope_spec, rope_t_spec, rope_t_spec,
                  _resident(conv_w, layer), _resident(conv_b, layer)],
        out_specs=out_specs,
        out_shape=out_shapes,
        scratch_shapes=[pltpu.VMEM((tm + 2 * SUBLANES, GROUP_W), F32)],
        compiler_params=_params("arbitrary"),
        name="mixer_inproj",
    )(x, gains, w_rows, w_cols_t, w_gate, cos, sin, cos_t, sin_t, conv_w, conv_b)


def _attn_kernel(lam_init, lam_ref, subg_ref, qt_ref, k_ref, vt_ref, o_ref,
                 m_sc, l_sc, acc_sc, s_sc):
    blk = ATT_BLOCK
    i = pl.program_id(2)
    qt = qt_ref[...]
    feat = lax.broadcasted_iota(jnp.int32, qt.shape, 0)
    zero = jnp.zeros_like(qt)
    qt_halves = (jnp.where(feat < DA_HEAD_DIM, qt, zero), jnp.where(feat >= DA_HEAD_DIM, qt, zero))

    def chunk_mask():
        key = lax.broadcasted_iota(jnp.int32, (blk, blk), 0) // CHUNK
        qry = lax.broadcasted_iota(jnp.int32, (blk, blk), 1) // CHUNK
        return key <= qry

    def key_block(j):
        start = pl.multiple_of(j * blk, blk)
        return k_ref[pl.ds(start, blk), :], vt_ref[:, pl.ds(start, blk)]

    def online_step(j, mask):
        kb, vb = key_block(j)
        for c in range(2):
            s = _dot(kb, qt_halves[c])
            if mask is not None:
                s = jnp.where(mask, s, -jnp.inf)
            m_old = m_sc[c]
            m_new = jnp.maximum(m_old, jnp.max(s, axis=0, keepdims=True))
            alpha = jnp.exp2(m_old - m_new)
            p = jnp.exp2(s - m_new)
            l_sc[c] = alpha * l_sc[c] + jnp.sum(p, axis=0, keepdims=True)
            acc_sc[c] = alpha * acc_sc[c] + _dot(vb, p.astype(BF16))
            m_sc[c] = m_new

    def scores(j, slot, mask=None):
        kb, _ = key_block(j)
        for c in range(2):
            s = _dot(kb, qt_halves[c])
            if mask is not None:
                s = jnp.where(mask, s, -jnp.inf)
                m_sc[c] = jnp.max(s, axis=0, keepdims=True)
            s_sc[slot, c] = s

    def accumulate(j, slot):
        _, vb = key_block(j)
        for c in range(2):
            p = jnp.exp2(s_sc[slot, c] - m_sc[c])
            l_sc[c] += jnp.sum(p, axis=0, keepdims=True)
            acc_sc[c] += _dot(vb, p.astype(BF16))

    l_sc[...] = jnp.zeros(l_sc.shape, F32)
    acc_sc[...] = jnp.zeros(acc_sc.shape, F32)
    scores(i, 0, chunk_mask())

    def pair(t, carry):
        j = 2 * t
        scores(j, 1)
        accumulate(jnp.where(t == 0, i, j - 1), 0)
        scores(j + 1, 0)
        accumulate(j, 1)
        return carry

    lax.fori_loop(0, i // 2, pair, 0)

    @pl.when(i % 2 == 1)
    def _():
        scores(i - 1, 1)
        accumulate(jnp.where(i == 1, i, i - 2), 0)
        accumulate(i - 1, 1)

    @pl.when(i % 2 == 0)
    def _():
        accumulate(jnp.where(i == 0, i, i - 1), 0)

    in_range = jnp.all(l_sc[...] < ATT_L_LIMIT)

    @pl.when(jnp.logical_not(in_range))
    def _():
        m_sc[...] = jnp.full(m_sc.shape, -jnp.inf, F32)
        l_sc[...] = jnp.zeros(l_sc.shape, F32)
        acc_sc[...] = jnp.zeros(acc_sc.shape, F32)
        online_step(i, chunk_mask())

        def body(j, carry):
            online_step(j, None)
            return carry

        lax.fori_loop(0, i, body, 0)

    lv = lam_ref[...]
    lam = (jnp.exp(jnp.sum(lv[0:1] * lv[1:2], axis=-1, keepdims=True))
           - jnp.exp(jnp.sum(lv[2:3] * lv[3:4], axis=-1, keepdims=True)) + lam_init)
    o_t = acc_sc[0] / l_sc[0] - lam * (acc_sc[1] / l_sc[1])
    o_ref[...] = (_rms(o_t.T, subg_ref[...]) * (1.0 - lam_init)).astype(BF16)


def _attn_call(qt, k, vt, lam_vecs, subln_g, layer, lam_init, batch, seq):
    tq = ATT_BLOCK
    nq = seq // tq
    qt_spec = pl.BlockSpec((LANES, tq), lambda b, h, i: (b * HEADS + h, i))
    k_spec = pl.BlockSpec((seq, LANES), lambda b, h, i: (b, h))
    vt_spec = pl.BlockSpec((LANES, seq), lambda b, h, i: (b * HEADS + h, 0))
    o_spec = pl.BlockSpec((tq, LANES), lambda b, h, i: (b * nq + i, h))
    return pl.pallas_call(
        functools.partial(_attn_kernel, lam_init),
        grid=(batch, HEADS, nq),
        in_specs=[_resident(lam_vecs, layer), _resident(subln_g, layer), qt_spec, k_spec, vt_spec],
        out_specs=o_spec,
        out_shape=jax.ShapeDtypeStruct(k.shape, BF16),
        scratch_shapes=[pltpu.VMEM((2, 1, tq), F32), pltpu.VMEM((2, 1, tq), F32),
                        pltpu.VMEM((2, LANES, tq), F32), pltpu.VMEM((2, 2, tq, tq), F32)],
        compiler_params=_params("arbitrary", "arbitrary", "arbitrary"),
        name="diff_attention",
    )(lam_vecs, subln_g, qt, k, vt)


def _split3(x):
    hi = x.astype(BF16)
    r = x - hi.astype(F32)
    mid = r.astype(BF16)
    lo = (r - mid.astype(F32)).astype(BF16)
    return hi, mid, lo


def _gate_kernel(raw_ref, bias_ref, g_ref, b_ref):
    log_i = raw_ref[0] + bias_ref[0]
    f = raw_ref[1] + bias_ref[1]
    log_f = jnp.minimum(f, 0.0) - jnp.log1p(jnp.exp(-jnp.abs(f)))
    n = log_f.shape[1]
    src = lax.broadcasted_iota(jnp.int32, (n, n), 0)
    dst = lax.broadcasted_iota(jnp.int32, (n, n), 1)
    tri = jnp.where(src <= dst, 1.0, 0.0).astype(BF16)
    hi, mid, lo = _split3(log_f)
    b = _dot(hi, tri) + _dot(mid, tri) + _dot(lo, tri)
    g_ref[...] = log_i - b
    b_ref[...] = b


def _gate_call(raw, bias):
    shape = jax.ShapeDtypeStruct(raw.shape[1:], F32)
    return pl.pallas_call(
        _gate_kernel,
        out_shape=(shape, shape),
        compiler_params=pltpu.CompilerParams(vmem_limit_bytes=VMEM_LIMIT),
        name="mlstm_gates",
    )(raw, bias)


def _mlstm_kernel(q_ref, k_ref, vt_ref, og_ref, grow_ref, gcol_ref, normg_ref, o_ref,
                  state_ref, m_ref):
    blk = q_ref.shape[0]

    @pl.when(pl.program_id(1) == 0)
    def _():
        state_ref[...] = jnp.zeros(state_ref.shape, F32)
        m_ref[...] = jnp.zeros(m_ref.shape, F32)

    src = lax.broadcasted_iota(jnp.int32, (blk, blk), 0)
    dst = lax.broadcasted_iota(jnp.int32, (blk, blk), 1)
    causal = src <= dst
    lane = lax.broadcasted_iota(jnp.int32, (blk, LANES), 1)
    ones_rows = jnp.ones((HEAD_V, blk), BF16)
    normg = normg_ref[...]

    for h in range(HEADS):
        pair = (h // 2) * LANES
        in_head = (lane // ML_QK) == (h % 2)
        q2 = q_ref[:, pair:pair + LANES]
        k2 = k_ref[:, pair:pair + LANES]
        kh = jnp.where(in_head, k2, jnp.zeros_like(k2))
        v_ext = jnp.concatenate([vt_ref[h * HEAD_V:(h + 1) * HEAD_V, :], ones_rows], axis=0)

        b_row = grow_ref[HEADS + h:HEADS + h + 1, :]
        g_src = jnp.broadcast_to(gcol_ref[:, h:h + 1], (blk, blk))
        m_prev = m_ref[h:h + 1, 0:1]

        run_max = jnp.max(jnp.where(causal, g_src, -jnp.inf), axis=0, keepdims=True)
        m_run = jnp.maximum(m_prev, run_max)
        s_t = _dot_nt(kh, q2)
        w_t = (jnp.where(causal, jnp.exp(g_src - m_run), 0.0) * s_t).astype(BF16)
        state = state_ref[h]
        nd = _dot(v_ext, w_t) + jnp.exp(m_prev - m_run) * _dot_nt(state.astype(BF16), q2)
        num = nd[0:HEAD_V]
        den = nd[HEAD_V:]
        hid = num / jnp.maximum(jnp.abs(den), jnp.exp(-(b_row + m_run)))
        hid = hid * lax.rsqrt(jnp.mean(hid * hid, axis=0, keepdims=True) + RMS_EPS)
        out = hid.T * normg * og_ref[:, h * HEAD_V:(h + 1) * HEAD_V].astype(F32)
        o_ref[:, h * HEAD_V:(h + 1) * HEAD_V] = out.astype(BF16)

        m_last = m_run[:, blk - 1:blk]
        kw = jnp.where(in_head, k2.astype(F32) * jnp.exp(g_src[:, 0:LANES] - m_last), 0.0)
        state_ref[h] = jnp.exp(m_prev - m_last) * state + _dot(v_ext, kw.astype(BF16))
        m_ref[h:h + 1, 0:1] = b_row[:, blk - 1:blk] + m_last


def _mlstm_call(q, k, vt, og, g_row, g_col, norm_g, layer, batch, seq):
    blk = ML_BLOCK
    nb = seq // blk
    row = lambda w: pl.BlockSpec((blk, w), lambda b, c: (b * nb + c, 0))
    return pl.pallas_call(
        _mlstm_kernel,
        grid=(batch, nb),
        in_specs=[row(HEADS * ML_QK), row(HEADS * ML_QK),
                  pl.BlockSpec((GROUP_W, blk), lambda b, c: (b, c)), row(GROUP_W),
                  pl.BlockSpec((None, None, 2 * HEADS, blk), lambda b, c: (b, c, 0, 0)),
                  pl.BlockSpec((None, None, blk, 2 * HEADS), lambda b, c: (b, c, 0, 0)),
                  _resident(norm_g, layer)],
        out_specs=row(GROUP_W),
        out_shape=jax.ShapeDtypeStruct(og.shape, BF16),
        scratch_shapes=[pltpu.VMEM((HEADS, 2 * HEAD_V, LANES), F32),
                        pltpu.VMEM((SUBLANES, LANES), F32)],
        compiler_params=_params("arbitrary", "arbitrary"),
        name="mlstm",
    )(q, k, vt, og, g_row, g_col, norm_g)


def _rope_tables(seq):
    inv_freq = ROPE_THETA ** (-jnp.arange(0, DA_HEAD_DIM, 2, dtype=F32) / DA_HEAD_DIM)
    ang = jnp.arange(seq, dtype=F32)[:, None] * inv_freq[None, :]
    cos, sin = jnp.cos(ang), jnp.sin(ang)
    reps = LANES // DA_HEAD_DIM
    cos_l = jnp.tile(jnp.concatenate([cos, cos], axis=1), (1, reps))
    sin_l = jnp.tile(jnp.concatenate([-sin, sin], axis=1), (1, reps))
    return cos_l, sin_l, cos.T, sin.T


def _mixer(x, p, layer, lam_init, rope, batch, seq):
    qt, k, vt, mq, mk, mv, mo, gates = _inproj_call(
        x, p["gains"], p["w_rows"], p["w_cols_t"], p["w_gate"], rope, p["conv_w"], p["conv_b"],
        layer, batch, seq)

    y_da = _attn_call(qt, k, vt, p["lam"], p["subln_g"], layer, lam_init, batch, seq)

    blk = ML_BLOCK
    nb = seq // blk
    raw = gates.reshape(batch, nb, blk, 2, HEADS).transpose(3, 0, 4, 1, 2)
    raw = raw.reshape(2, batch * HEADS * nb, blk)
    g, b = _gate_call(raw, p["gate_bias"][layer])
    gb = jnp.stack([g.reshape(batch, HEADS, nb, blk), b.reshape(batch, HEADS, nb, blk)], axis=1)
    gb = gb.reshape(batch, 2 * HEADS, nb, blk)
    g_row = gb.transpose(0, 2, 1, 3)
    g_col = gb.transpose(0, 2, 3, 1)
    y_ml = _mlstm_call(mq, mk, mv, mo, g_row, g_col, p["norm_g"], layer, batch, seq)
    return y_da, y_ml


def _prepare(norm_gains, mix_w_in, mix_w_out, da_lambda, da_subln_g, ml_conv_w, ml_conv_b,
             ml_gate_b, ml_norm_g, batch, seq):
    w_bf = mix_w_in.astype(BF16)
    nb = seq // ML_BLOCK
    bias = jnp.broadcast_to(ml_gate_b.reshape(DEPTH, 2, 1, HEADS, 1), (DEPTH, 2, batch, HEADS, nb))
    return {
        "gains": norm_gains,
        "w_rows": jnp.concatenate([w_bf[:, :, GROUP_W:2 * GROUP_W],
                                   w_bf[:, :, 3 * GROUP_W:4 * GROUP_W],
                                   w_bf[:, :, 5 * GROUP_W:N_MAIN]], axis=2),
        "w_cols_t": jnp.concatenate([w_bf[:, :, 0:GROUP_W], w_bf[:, :, 2 * GROUP_W:3 * GROUP_W],
                                     w_bf[:, :, 4 * GROUP_W:5 * GROUP_W]],
                                    axis=2).transpose(0, 2, 1),
        "w_gate": jnp.pad(w_bf[:, :, N_MAIN:], ((0, 0), (0, 0), (0, LANES - N_GATES))),
        "w_mix_out": mix_w_out.astype(BF16),
        "lam": da_lambda,
        "subln_g": da_subln_g[:, None, :],
        "conv_w": ml_conv_w,
        "conv_b": ml_conv_b[:, None, :],
        "gate_bias": bias.reshape(DEPTH, 2, batch * HEADS * nb, 1),
        "norm_g": ml_norm_g[:, None, :],
    }


def kernel(x, ffn_w_in, ffn_w_out, norm_gains, mix_w_in, mix_w_out, da_lambda, da_subln_g,
           ml_conv_w, ml_conv_b, ml_gate_b, ml_norm_g):
    batch, seq, d = x.shape
    assert d == D_MODEL and seq % FFN_ROWS == 0 and seq % ML_BLOCK == 0
    assert seq % ATT_BLOCK == 0
    rope = _rope_tables(seq)
    p = _prepare(norm_gains, mix_w_in, mix_w_out, da_lambda, da_subln_g, ml_conv_w, ml_conv_b,
                 ml_gate_b, ml_norm_g, batch, seq)
    w_in = ffn_w_in.astype(BF16)
    w_out = ffn_w_out.astype(BF16)
    h = x.reshape(batch * seq, d)
    for l in range(DEPTH):
        h = _ffn_call(h, norm_gains, w_in, w_out, l)
        lam_init = 0.8 - 0.6 * math.exp(-0.3 * l)
        y_da, y_ml = _mixer(h, p, l, lam_init, rope, batch, seq)
        h = _ffn_call(h, norm_gains, w_in, w_out, l, mix=(y_da, y_ml, p["w_mix_out"]))
    return h.reshape(batch, seq, d)
```

```python
import functools
import math

import jax
import jax.numpy as jnp
from jax import lax
from jax.experimental import pallas as pl
from jax.experimental.pallas import tpu as pltpu

D_MODEL = 1024
DEPTH = 2
CHUNK = 64
ROPE_THETA = 10000.0
RMS_EPS = 1e-6
D_FF = 2816
HEADS = 4
DA_HEAD_DIM = 64
HEAD_V = 128
GROUP_W = HEADS * HEAD_V
ML_QK = 64
ML_CONV = 4
N_MAIN = 6 * GROUP_W
N_GATES = 2 * HEADS
Q_SCALE = DA_HEAD_DIM ** -0.5 * math.log2(math.e)

LANES = 128
SUBLANES = 8
VMEM_LIMIT = 52 * 1024 * 1024

FFN_ROWS = 512
FFN_COLS = 256
ROW_SPLIT = 2
ATT_BLOCK = 512
ATT_L_LIMIT = 2.0 ** 64
ML_BLOCK = 256
ML_STEP_CHUNKS = 4

BF16 = jnp.bfloat16
F32 = jnp.float32


def _dot(a, b):
    return jnp.dot(a, b, preferred_element_type=F32)


def _dot_nt(a, b):
    return lax.dot_general(a, b, (((1,), (1,)), ((), ())), preferred_element_type=F32)


def _dot_tn(a, b):
    return lax.dot_general(a, b, (((0,), (0,)), ((), ())), preferred_element_type=F32)


def _rms(x, g):
    return x * lax.rsqrt(jnp.mean(x * x, axis=-1, keepdims=True) + RMS_EPS) * g


def _params(*sem):
    return pltpu.CompilerParams(dimension_semantics=sem, vmem_limit_bytes=VMEM_LIMIT)


def _resident(arr, *lead):
    tail = arr.shape[len(lead):]
    index = tuple(lead) + (0,) * len(tail)
    return pl.BlockSpec((None,) * len(lead) + tail, lambda *_: index,
                        pipeline_mode=pl.Buffered(1))


def _ffn_body(x, g_pre, g_post, win_ref, wout_ref):
    xn = _rms(x, g_pre).astype(BF16)
    acc = jnp.zeros(x.shape, F32)
    for c in range(D_FF // FFN_COLS):
        lo = c * FFN_COLS
        gate = _dot(xn, win_ref[:, lo:lo + FFN_COLS])
        up = _dot(xn, win_ref[:, D_FF + lo:D_FF + lo + FFN_COLS])
        act = (gate * jax.nn.sigmoid(gate) * up).astype(BF16)
        acc = acc + _dot(act, wout_ref[lo:lo + FFN_COLS, :])
    return x + 0.5 * _rms(acc, g_post)


def _ffn_kernel(x_ref, gains_ref, win_ref, wout_ref, o_ref):
    o_ref[...] = _ffn_body(x_ref[...], gains_ref[0:1], gains_ref[1:2], win_ref, wout_ref)


def _mix_ffn_kernel(x_ref, yda_ref, yml_ref, wmix_ref, gains_ref, win_ref, wout_ref, o_ref):
    h = _dot(yda_ref[...], wmix_ref[0:GROUP_W, :]) + _dot(yml_ref[...], wmix_ref[GROUP_W:, :])
    x = x_ref[...] + _rms(h, gains_ref[3:4])
    o_ref[...] = _ffn_body(x, gains_ref[4:5], gains_ref[5:6], win_ref, wout_ref)


def _ffn_call(x, gains, w_in, w_out, layer, mix=None):
    rows = x.shape[0]
    tm = FFN_ROWS
    which = 0 if mix is None else 1
    row_spec = pl.BlockSpec((tm, D_MODEL), lambda i: (i, 0))
    half_spec = pl.BlockSpec((tm, GROUP_W), lambda i: (i, 0))
    weights = [_resident(gains, layer), _resident(w_in, layer, which),
               _resident(w_out, layer, which)]
    if mix is None:
        kern, ins, specs = _ffn_kernel, (x, gains, w_in, w_out), [row_spec] + weights
    else:
        y_da, y_ml, w_mix = mix
        kern = _mix_ffn_kernel
        ins = (x, y_da, y_ml, w_mix, gains, w_in, w_out)
        specs = [row_spec, half_spec, half_spec, _resident(w_mix, layer)] + weights
    return pl.pallas_call(
        kern,
        grid=(rows // tm,),
        in_specs=specs,
        out_specs=row_spec,
        out_shape=jax.ShapeDtypeStruct(x.shape, F32),
        compiler_params=_params("arbitrary"),
        name="ffn" if mix is None else "mix_ffn",
    )(*ins)


def _rope(t, cos, sin_signed, first_half):
    swapped = jnp.where(first_half, pltpu.roll(t, LANES - 32, axis=1), pltpu.roll(t, 32, axis=1))
    return t * cos + swapped * sin_signed


def _rope_rows(t, cos, sin):
    half = DA_HEAD_DIM // 2
    a, b = t[0:half], t[half:]
    return jnp.concatenate([a * cos - b * sin, b * cos + a * sin], axis=0)


def _inproj_kernel(tiles_per_seq, x_ref, gain_ref, w_ref, wt_ref, cos_ref, sin_ref,
                   cost_ref, sint_ref, convw_ref, convb_ref,
                   qt_ref, k_ref, vt_ref, mq_ref, mk_ref, mv_ref, mo_ref, gates_ref,
                   conv_ref):
    tm = x_ref.shape[0]
    halo = SUBLANES
    sub = tm // ROW_SPLIT

    @pl.when(pl.program_id(0) % tiles_per_seq == 0)
    def _():
        conv_ref[0:halo, :] = jnp.zeros((halo, GROUP_W), F32)

    lane = lax.broadcasted_iota(jnp.int32, (sub, LANES), 1)
    first_half = (lane % 64) < 32

    for part in range(ROW_SPLIT):
        r0 = part * sub
        rows = slice(r0, r0 + sub)
        xn = _rms(x_ref[rows, :], gain_ref[2:3]).astype(BF16)

        qv_t = _dot_nt(wt_ref[...], xn)
        gates_ref[:, rows] = qv_t[3 * GROUP_W:3 * GROUP_W + N_GATES, :]
        cos_t = cost_ref[:, rows]
        sin_t = sint_ref[:, rows]
        for r in range(GROUP_W // DA_HEAD_DIM):
            lo = r * DA_HEAD_DIM
            qt_ref[lo:lo + DA_HEAD_DIM, rows] = (
                _rope_rows(qv_t[lo:lo + DA_HEAD_DIM], cos_t, sin_t) * Q_SCALE).astype(BF16)
        vt_ref[:, rows] = qv_t[GROUP_W:2 * GROUP_W, :].astype(BF16)
        mv_ref[:, rows] = qv_t[2 * GROUP_W:3 * GROUP_W, :].astype(BF16)

        cos = cos_ref[rows, :]
        sin = sin_ref[rows, :]
        kk = _dot(xn, w_ref[:, 0:GROUP_W])
        for h in range(HEADS):
            lo = h * LANES
            k_ref[rows, lo:lo + LANES] = _rope(kk[:, lo:lo + LANES], cos, sin, first_half).astype(BF16)

        base = halo + r0
        conv_ref[base:base + sub, :] = _dot(xn, w_ref[:, GROUP_W:2 * GROUP_W])
        y = convb_ref[...] + convw_ref[ML_CONV - 1:ML_CONV, :] * conv_ref[base:base + sub, :]
        for j in range(ML_CONV - 1):
            back = ML_CONV - 1 - j
            y = y + convw_ref[j:j + 1, :] * conv_ref[base - back:base - back + sub, :]
        y = y * jax.nn.sigmoid(y)
        mq_ref[rows, :] = (y[:, 0:HEADS * ML_QK] * (ML_QK ** -0.5)).astype(BF16)
        mk_ref[rows, :] = y[:, HEADS * ML_QK:].astype(BF16)

        mo_ref[rows, :] = jax.nn.sigmoid(_dot(xn, w_ref[:, 2 * GROUP_W:3 * GROUP_W])).astype(BF16)

    conv_ref[0:halo, :] = conv_ref[tm:tm + halo, :]


def _inproj_call(x, gains, w_rows, w_cols_t, rope, conv_w, conv_b, layer, batch, seq):
    rows = x.shape[0]
    tm = FFN_ROWS
    tiles_per_seq = seq // tm
    row = lambda w: pl.BlockSpec((tm, w), lambda i: (i, 0))
    rope_spec = pl.BlockSpec((tm, LANES), lambda i: (i % tiles_per_seq, 0))
    rope_t_spec = pl.BlockSpec((DA_HEAD_DIM // 2, tm), lambda i: (0, i % tiles_per_seq))
    t_spec = pl.BlockSpec((GROUP_W, tm), lambda i: (i // tiles_per_seq, i % tiles_per_seq))
    out_shapes = (
        jax.ShapeDtypeStruct((batch * GROUP_W, seq), BF16),
        jax.ShapeDtypeStruct((rows, GROUP_W), BF16),
        jax.ShapeDtypeStruct((batch * GROUP_W, seq), BF16),
        jax.ShapeDtypeStruct((rows, HEADS * ML_QK), BF16),
        jax.ShapeDtypeStruct((rows, HEADS * ML_QK), BF16),
        jax.ShapeDtypeStruct((batch * GROUP_W, seq), BF16),
        jax.ShapeDtypeStruct((rows, GROUP_W), BF16),
        jax.ShapeDtypeStruct((batch * N_GATES, seq), F32),
    )
    gate_spec = pl.BlockSpec((N_GATES, tm), lambda i: (i // tiles_per_seq, i % tiles_per_seq))
    out_specs = (t_spec, row(GROUP_W), t_spec, row(HEADS * ML_QK), row(HEADS * ML_QK),
                 t_spec, row(GROUP_W), gate_spec)
    cos, sin, cos_t, sin_t = rope
    return pl.pallas_call(
        functools.partial(_inproj_kernel, tiles_per_seq),
        grid=(rows // tm,),
        in_specs=[row(D_MODEL), _resident(gains, layer), _resident(w_rows, layer),
                  _resident(w_cols_t, layer),
                  rope_spec, rope_spec, rope_t_spec, rope_t_spec,
                  _resident(conv_w, layer), _resident(conv_b, layer)],
        out_specs=out_specs,
        out_shape=out_shapes,
        scratch_shapes=[pltpu.VMEM((tm + 2 * SUBLANES, GROUP_W), F32)],
        compiler_params=_params("arbitrary"),
        name="mixer_inproj",
    )(x, gains, w_rows, w_cols_t, cos, sin, cos_t, sin_t, conv_w, conv_b)


def _attn_kernel(lam_init, lam_ref, subg_ref, qt_ref, k_ref, vt_ref, o_ref,
                 m_sc, l_sc, acc_sc, s_sc):
    blk = ATT_BLOCK
    n_blocks = qt_ref.shape[1] // blk

    def query_block(i, carry):
        q_start = pl.multiple_of(i * blk, blk)
        qt = qt_ref[:, pl.ds(q_start, blk)]
        feat = lax.broadcasted_iota(jnp.int32, qt.shape, 0)
        zero = jnp.zeros_like(qt)
        qt_halves = (jnp.where(feat < DA_HEAD_DIM, qt, zero), jnp.where(feat >= DA_HEAD_DIM, qt, zero))

        def chunk_mask():
            key = lax.broadcasted_iota(jnp.int32, (blk, blk), 0) // CHUNK
            qry = lax.broadcasted_iota(jnp.int32, (blk, blk), 1) // CHUNK
            return key <= qry

        def key_block(j):
            start = pl.multiple_of(j * blk, blk)
            return k_ref[pl.ds(start, blk), :], vt_ref[:, pl.ds(start, blk)]

        def online_step(j, mask):
            kb, vb = key_block(j)
            for c in range(2):
                s = _dot(kb, qt_halves[c])
                if mask is not None:
                    s = jnp.where(mask, s, -jnp.inf)
                m_old = m_sc[c]
                m_new = jnp.maximum(m_old, jnp.max(s, axis=0, keepdims=True))
                alpha = jnp.exp2(m_old - m_new)
                p = jnp.exp2(s - m_new)
                l_sc[c] = alpha * l_sc[c] + jnp.sum(p, axis=0, keepdims=True)
                acc_sc[c] = alpha * acc_sc[c] + _dot(vb, p.astype(BF16))
                m_sc[c] = m_new

        def scores(j, slot, mask=None):
            kb, _ = key_block(j)
            for c in range(2):
                s = _dot(kb, qt_halves[c])
                if mask is not None:
                    s = jnp.where(mask, s, -jnp.inf)
                    m_sc[c] = jnp.max(s, axis=0, keepdims=True)
                s_sc[slot, c] = s

        def accumulate(j, slot):
            _, vb = key_block(j)
            for c in range(2):
                p = jnp.exp2(s_sc[slot, c] - m_sc[c])
                l_sc[c] += jnp.sum(p, axis=0, keepdims=True)
                acc_sc[c] += _dot(vb, p.astype(BF16))

        l_sc[...] = jnp.zeros(l_sc.shape, F32)
        acc_sc[...] = jnp.zeros(acc_sc.shape, F32)
        scores(i, 0, chunk_mask())

        def pair(t, carry):
            j = 2 * t
            scores(j, 1)
            accumulate(jnp.where(t == 0, i, j - 1), 0)
            scores(j + 1, 0)
            accumulate(j, 1)
            return carry

        lax.fori_loop(0, i // 2, pair, 0)

        @pl.when(i % 2 == 1)
        def _():
            scores(i - 1, 1)
            accumulate(jnp.where(i == 1, i, i - 2), 0)
            accumulate(i - 1, 1)

        @pl.when(i % 2 == 0)
        def _():
            accumulate(jnp.where(i == 0, i, i - 1), 0)

        in_range = jnp.all(l_sc[...] < ATT_L_LIMIT)

        @pl.when(jnp.logical_not(in_range))
        def _():
            m_sc[...] = jnp.full(m_sc.shape, -jnp.inf, F32)
            l_sc[...] = jnp.zeros(l_sc.shape, F32)
            acc_sc[...] = jnp.zeros(acc_sc.shape, F32)
            online_step(i, chunk_mask())

            def body(j, carry):
                online_step(j, None)
                return carry

            lax.fori_loop(0, i, body, 0)

        lv = lam_ref[...]
        lam = (jnp.exp(jnp.sum(lv[0:1] * lv[1:2], axis=-1, keepdims=True))
               - jnp.exp(jnp.sum(lv[2:3] * lv[3:4], axis=-1, keepdims=True)) + lam_init)
        o_t = acc_sc[0] / l_sc[0] - lam * (acc_sc[1] / l_sc[1])
        o_ref[pl.ds(q_start, blk), :] = (
            _rms(o_t.T, subg_ref[...]) * (1.0 - lam_init)).astype(BF16)
        return carry

    lax.fori_loop(0, n_blocks, query_block, 0)


def _attn_call(qt, k, vt, lam_vecs, subln_g, layer, lam_init, batch, seq):
    tq = ATT_BLOCK
    t_spec = pl.BlockSpec((LANES, seq), lambda b, h: (b * HEADS + h, 0))
    row_spec = pl.BlockSpec((seq, LANES), lambda b, h: (b, h))
    return pl.pallas_call(
        functools.partial(_attn_kernel, lam_init),
        grid=(batch, HEADS),
        in_specs=[_resident(lam_vecs, layer), _resident(subln_g, layer), t_spec, row_spec, t_spec],
        out_specs=row_spec,
        out_shape=jax.ShapeDtypeStruct(k.shape, BF16),
        scratch_shapes=[pltpu.VMEM((2, 1, tq), F32), pltpu.VMEM((2, 1, tq), F32),
                        pltpu.VMEM((2, LANES, tq), F32), pltpu.VMEM((2, 2, tq, tq), F32)],
        compiler_params=_params("arbitrary", "arbitrary"),
        name="diff_attention",
    )(lam_vecs, subln_g, qt, k, vt)


def _split3(x):
    hi = x.astype(BF16)
    r = x - hi.astype(F32)
    mid = r.astype(BF16)
    lo = (r - mid.astype(F32)).astype(BF16)
    return hi, mid, lo


def _chunk_gates(raw, bias, tri):
    z = raw + bias
    log_f = jnp.minimum(z, 0.0) - jnp.log1p(jnp.exp(-jnp.abs(z)))
    hi, mid, lo = _split3(log_f)
    b = (_dot(hi, tri) + _dot(mid, tri) + _dot(lo, tri))[HEADS:]
    return z[0:HEADS] - b, b


def _mlstm_kernel(q_ref, k_ref, vt_ref, og_ref, gates_ref, bias_ref, normg_ref, o_ref,
                  state_ref, m_ref):
    blk = ML_BLOCK

    @pl.when(pl.program_id(1) == 0)
    def _():
        state_ref[...] = jnp.zeros(state_ref.shape, F32)
        m_ref[...] = jnp.zeros(m_ref.shape, F32)

    src = lax.broadcasted_iota(jnp.int32, (blk, blk), 0)
    dst = lax.broadcasted_iota(jnp.int32, (blk, blk), 1)
    causal = src <= dst
    lane = lax.broadcasted_iota(jnp.int32, (blk, LANES), 1)
    ones_rows = jnp.ones((HEAD_V, blk), BF16)
    tri = jnp.where(causal, 1.0, 0.0).astype(BF16)
    normg = normg_ref[...]
    bias = bias_ref[...]

    for chunk in range(ML_STEP_CHUNKS):
        rows = slice(chunk * blk, (chunk + 1) * blk)
        g_all, b_all = _chunk_gates(gates_ref[:, rows], bias, tri)
        for h in range(HEADS):
            pair = (h // 2) * LANES
            in_head = (lane // ML_QK) == (h % 2)
            q2 = q_ref[rows, pair:pair + LANES]
            k2 = k_ref[rows, pair:pair + LANES]
            kh = jnp.where(in_head, k2, jnp.zeros_like(k2))
            v_ext = jnp.concatenate([vt_ref[h * HEAD_V:(h + 1) * HEAD_V, rows], ones_rows], axis=0)

            b_row = b_all[h:h + 1]
            g_lanes = jnp.broadcast_to(g_all[h:h + 1], (LANES, blk)).T
            g_src = jnp.concatenate([g_lanes] * (blk // LANES), axis=1)
            m_prev = m_ref[h:h + 1, 0:1]

            run_max = jnp.max(jnp.where(causal, g_src, -jnp.inf), axis=0, keepdims=True)
            m_run = jnp.maximum(m_prev, run_max)
            s_t = _dot_nt(kh, q2)
            w_t = (jnp.where(causal, jnp.exp(g_src - m_run), 0.0) * s_t).astype(BF16)
            state = state_ref[h]
            nd = _dot(v_ext, w_t) + jnp.exp(m_prev - m_run) * _dot_nt(state.astype(BF16), q2)
            num = nd[0:HEAD_V]
            den = nd[HEAD_V:]
            hid = num / jnp.maximum(jnp.abs(den), jnp.exp(-(b_row + m_run)))
            hid = hid * lax.rsqrt(jnp.mean(hid * hid, axis=0, keepdims=True) + RMS_EPS)
            out = hid.T * normg * og_ref[rows, h * HEAD_V:(h + 1) * HEAD_V].astype(F32)
            o_ref[rows, h * HEAD_V:(h + 1) * HEAD_V] = out.astype(BF16)

            m_last = m_run[:, blk - 1:blk]
            kw = jnp.where(in_head, k2.astype(F32) * jnp.exp(g_lanes - m_last), 0.0)
            state_ref[h] = jnp.exp(m_prev - m_last) * state + _dot(v_ext, kw.astype(BF16))
            m_ref[h:h + 1, 0:1] = b_row[:, blk - 1:blk] + m_last


def _mlstm_call(q, k, vt, og, gates_t, gate_b, norm_g, layer, batch, seq):
    step = ML_STEP_CHUNKS * ML_BLOCK
    ns = seq // step
    row = lambda w: pl.BlockSpec((step, w), lambda b, c: (b * ns + c, 0))
    return pl.pallas_call(
        _mlstm_kernel,
        grid=(batch, ns),
        in_specs=[row(HEADS * ML_QK), row(HEADS * ML_QK),
                  pl.BlockSpec((GROUP_W, step), lambda b, c: (b, c)), row(GROUP_W),
                  pl.BlockSpec((N_GATES, step), lambda b, c: (b, c)),
                  _resident(gate_b, layer), _resident(norm_g, layer)],
        out_specs=row(GROUP_W),
        out_shape=jax.ShapeDtypeStruct(og.shape, BF16),
        scratch_shapes=[pltpu.VMEM((HEADS, 2 * HEAD_V, LANES), F32),
                        pltpu.VMEM((SUBLANES, LANES), F32)],
        compiler_params=_params("arbitrary", "arbitrary"),
        name="mlstm",
    )(q, k, vt, og, gates_t, gate_b, norm_g)


def _rope_tables(seq):
    inv_freq = ROPE_THETA ** (-jnp.arange(0, DA_HEAD_DIM, 2, dtype=F32) / DA_HEAD_DIM)
    ang = jnp.arange(seq, dtype=F32)[:, None] * inv_freq[None, :]
    cos, sin = jnp.cos(ang), jnp.sin(ang)
    reps = LANES // DA_HEAD_DIM
    cos_l = jnp.tile(jnp.concatenate([cos, cos], axis=1), (1, reps))
    sin_l = jnp.tile(jnp.concatenate([-sin, sin], axis=1), (1, reps))
    return cos_l, sin_l, cos.T, sin.T


def _mixer(x, p, layer, lam_init, rope, batch, seq):
    qt, k, vt, mq, mk, mv, mo, gates_t = _inproj_call(
        x, p["gains"], p["w_rows"], p["w_cols_t"], rope, p["conv_w"], p["conv_b"],
        layer, batch, seq)
    y_da = _attn_call(qt, k, vt, p["lam"], p["subln_g"], layer, lam_init, batch, seq)
    y_ml = _mlstm_call(mq, mk, mv, mo, gates_t, p["gate_b"], p["norm_g"], layer, batch, seq)
    return y_da, y_ml


def _prepare(norm_gains, mix_w_in, mix_w_out, da_lambda, da_subln_g, ml_conv_w, ml_conv_b,
             ml_gate_b, ml_norm_g):
    w_bf = mix_w_in.astype(BF16)
    gate_pad = jnp.zeros((DEPTH, D_MODEL, LANES - N_GATES), F32)
    return {
        "gains": norm_gains,
        "w_rows": jnp.concatenate([w_bf[:, :, GROUP_W:2 * GROUP_W],
                                   w_bf[:, :, 3 * GROUP_W:4 * GROUP_W],
                                   w_bf[:, :, 5 * GROUP_W:N_MAIN]], axis=2),
        "w_cols_t": jnp.concatenate([mix_w_in[:, :, 0:GROUP_W],
                                     mix_w_in[:, :, 2 * GROUP_W:3 * GROUP_W],
                                     mix_w_in[:, :, 4 * GROUP_W:5 * GROUP_W],
                                     mix_w_in[:, :, N_MAIN:], gate_pad],
                                    axis=2).transpose(0, 2, 1).astype(BF16),
        "w_mix_out": mix_w_out.astype(BF16),
        "lam": da_lambda,
        "subln_g": da_subln_g[:, None, :],
        "conv_w": ml_conv_w,
        "conv_b": ml_conv_b[:, None, :],
        "gate_b": ml_gate_b[:, :, None],
        "norm_g": ml_norm_g[:, None, :],
    }


def kernel(x, ffn_w_in, ffn_w_out, norm_gains, mix_w_in, mix_w_out, da_lambda, da_subln_g,
           ml_conv_w, ml_conv_b, ml_gate_b, ml_norm_g):
    batch, seq, d = x.shape
    assert d == D_MODEL and seq % FFN_ROWS == 0 and seq % (ML_BLOCK * ML_STEP_CHUNKS) == 0
    assert seq % ATT_BLOCK == 0
    rope = _rope_tables(seq)
    p = _prepare(norm_gains, mix_w_in, mix_w_out, da_lambda, da_subln_g, ml_conv_w, ml_conv_b,
                 ml_gate_b, ml_norm_g)
    w_in = ffn_w_in.astype(BF16)
    w_out = ffn_w_out.astype(BF16)
    h = x.reshape(batch * seq, d)
    for l in range(DEPTH):
        h = _ffn_call(h, norm_gains, w_in, w_out, l)
        lam_init = 0.8 - 0.6 * math.exp(-0.3 * l)
        y_da, y_ml = _mixer(h, p, l, lam_init, rope, batch, seq)
        h = _ffn_call(h, norm_gains, w_in, w_out, l, mix=(y_da, y_ml, p["w_mix_out"]))
    return h.reshape(batch, seq, d)
```

```python
import functools
import math

import jax
import jax.numpy as jnp
from jax import lax
from jax.experimental import pallas as pl
from jax.experimental.pallas import tpu as pltpu

D_MODEL = 1024
DEPTH = 2
CHUNK = 64
ROPE_THETA = 10000.0
RMS_EPS = 1e-6
D_FF = 2816
HEADS = 4
DA_HEAD_DIM = 64
HEAD_V = 128
GROUP_W = HEADS * HEAD_V
ML_QK = 64
ML_CONV = 4
N_MAIN = 6 * GROUP_W
N_GATES = 2 * HEADS
Q_SCALE = DA_HEAD_DIM ** -0.5 * math.log2(math.e)

LANES = 128
SUBLANES = 8
VMEM_LIMIT = 52 * 1024 * 1024

FFN_ROWS = 512
FFN_COLS = 256
ROW_SPLIT = 2
ATT_BLOCK = 512
ATT_L_LIMIT = 2.0 ** 64
ML_BLOCK = 256
ML_STEP_CHUNKS = 4

BF16 = jnp.bfloat16
F32 = jnp.float32


def _dot(a, b):
    return jnp.dot(a, b, preferred_element_type=F32)


def _dot_nt(a, b):
    return lax.dot_general(a, b, (((1,), (1,)), ((), ())), preferred_element_type=F32)


def _dot_tn(a, b):
    return lax.dot_general(a, b, (((0,), (0,)), ((), ())), preferred_element_type=F32)


def _rms(x, g):
    return x * lax.rsqrt(jnp.mean(x * x, axis=-1, keepdims=True) + RMS_EPS) * g


def _params(*sem):
    return pltpu.CompilerParams(dimension_semantics=sem, vmem_limit_bytes=VMEM_LIMIT)


def _resident(arr, *lead):
    tail = arr.shape[len(lead):]
    index = tuple(lead) + (0,) * len(tail)
    return pl.BlockSpec((None,) * len(lead) + tail, lambda *_: index,
                        pipeline_mode=pl.Buffered(1))


def _ffn_body(x, g_pre, g_post, win_ref, wout_ref):
    xn = _rms(x, g_pre).astype(BF16)
    acc = jnp.zeros(x.shape, F32)
    for c in range(D_FF // FFN_COLS):
        lo = c * FFN_COLS
        gate = _dot(xn, win_ref[:, lo:lo + FFN_COLS])
        up = _dot(xn, win_ref[:, D_FF + lo:D_FF + lo + FFN_COLS])
        act = (gate * jax.nn.sigmoid(gate) * up).astype(BF16)
        acc = acc + _dot(act, wout_ref[lo:lo + FFN_COLS, :])
    return x + 0.5 * _rms(acc, g_post)


def _ffn_kernel(x_ref, gains_ref, win_ref, wout_ref, o_ref):
    o_ref[...] = _ffn_body(x_ref[...], gains_ref[0:1], gains_ref[1:2], win_ref, wout_ref)


def _mix_ffn_kernel(x_ref, yda_ref, yml_ref, wmix_ref, gains_ref, win_ref, wout_ref, o_ref):
    h = _dot(yda_ref[...], wmix_ref[0:GROUP_W, :]) + _dot(yml_ref[...], wmix_ref[GROUP_W:, :])
    x = x_ref[...] + _rms(h, gains_ref[3:4])
    o_ref[...] = _ffn_body(x, gains_ref[4:5], gains_ref[5:6], win_ref, wout_ref)


def _ffn_call(x, gains, w_in, w_out, layer, mix=None):
    rows = x.shape[0]
    tm = FFN_ROWS
    which = 0 if mix is None else 1
    row_spec = pl.BlockSpec((tm, D_MODEL), lambda i: (i, 0))
    half_spec = pl.BlockSpec((tm, GROUP_W), lambda i: (i, 0))
    weights = [_resident(gains, layer), _resident(w_in, layer, which),
               _resident(w_out, layer, which)]
    if mix is None:
        kern, ins, specs = _ffn_kernel, (x, gains, w_in, w_out), [row_spec] + weights
    else:
        y_da, y_ml, w_mix = mix
        kern = _mix_ffn_kernel
        ins = (x, y_da, y_ml, w_mix, gains, w_in, w_out)
        specs = [row_spec, half_spec, half_spec, _resident(w_mix, layer)] + weights
    return pl.pallas_call(
        kern,
        grid=(rows // tm,),
        in_specs=specs,
        out_specs=row_spec,
        out_shape=jax.ShapeDtypeStruct(x.shape, F32),
        compiler_params=_params("arbitrary"),
        name="ffn" if mix is None else "mix_ffn",
    )(*ins)


def _rope(t, cos, sin_signed, first_half):
    swapped = jnp.where(first_half, pltpu.roll(t, LANES - 32, axis=1), pltpu.roll(t, 32, axis=1))
    return t * cos + swapped * sin_signed


def _rope_rows(t, cos, sin):
    half = DA_HEAD_DIM // 2
    a, b = t[0:half], t[half:]
    return jnp.concatenate([a * cos - b * sin, b * cos + a * sin], axis=0)


def _inproj_kernel(tiles_per_seq, x_ref, gain_ref, w_ref, wt_ref, cos_ref, sin_ref,
                   cost_ref, sint_ref, convw_ref, convb_ref,
                   qt_ref, k_ref, vt_ref, mq_ref, mk_ref, mv_ref, mo_ref, gates_ref,
                   conv_ref):
    tm = x_ref.shape[0]
    halo = SUBLANES
    sub = tm // ROW_SPLIT

    @pl.when(pl.program_id(0) % tiles_per_seq == 0)
    def _():
        conv_ref[0:halo, :] = jnp.zeros((halo, GROUP_W), F32)

    lane = lax.broadcasted_iota(jnp.int32, (sub, LANES), 1)
    first_half = (lane % 64) < 32

    for part in range(ROW_SPLIT):
        r0 = part * sub
        rows = slice(r0, r0 + sub)
        xn = _rms(x_ref[rows, :], gain_ref[2:3]).astype(BF16)

        qv_t = _dot_nt(wt_ref[...], xn)
        gates_ref[:, rows] = qv_t[3 * GROUP_W:3 * GROUP_W + N_GATES, :]
        cos_t = cost_ref[:, rows]
        sin_t = sint_ref[:, rows]
        for r in range(GROUP_W // DA_HEAD_DIM):
            lo = r * DA_HEAD_DIM
            qt_ref[lo:lo + DA_HEAD_DIM, rows] = (
                _rope_rows(qv_t[lo:lo + DA_HEAD_DIM], cos_t, sin_t) * Q_SCALE).astype(BF16)
        vt_ref[:, rows] = qv_t[GROUP_W:2 * GROUP_W, :].astype(BF16)
        mv_ref[:, rows] = qv_t[2 * GROUP_W:3 * GROUP_W, :].astype(BF16)

        cos = cos_ref[rows, :]
        sin = sin_ref[rows, :]
        kk = _dot(xn, w_ref[:, 0:GROUP_W])
        for h in range(HEADS):
            lo = h * LANES
            k_ref[rows, lo:lo + LANES] = _rope(kk[:, lo:lo + LANES], cos, sin, first_half).astype(BF16)

        base = halo + r0
        conv_ref[base:base + sub, :] = _dot(xn, w_ref[:, GROUP_W:2 * GROUP_W])
        y = convb_ref[...] + convw_ref[ML_CONV - 1:ML_CONV, :] * conv_ref[base:base + sub, :]
        for j in range(ML_CONV - 1):
            back = ML_CONV - 1 - j
            y = y + convw_ref[j:j + 1, :] * conv_ref[base - back:base - back + sub, :]
        y = y * jax.nn.sigmoid(y)
        mq_ref[rows, :] = (y[:, 0:HEADS * ML_QK] * (ML_QK ** -0.5)).astype(BF16)
        mk_ref[rows, :] = y[:, HEADS * ML_QK:].astype(BF16)

        mo_ref[rows, :] = jax.nn.sigmoid(_dot(xn, w_ref[:, 2 * GROUP_W:3 * GROUP_W])).astype(BF16)

    conv_ref[0:halo, :] = conv_ref[tm:tm + halo, :]


def _inproj_call(x, gains, w_rows, w_cols_t, rope, conv_w, conv_b, layer, batch, seq):
    rows = x.shape[0]
    tm = FFN_ROWS
    tiles_per_seq = seq // tm
    row = lambda w: pl.BlockSpec((tm, w), lambda i: (i, 0))
    rope_spec = pl.BlockSpec((tm, LANES), lambda i: (i % tiles_per_seq, 0))
    rope_t_spec = pl.BlockSpec((DA_HEAD_DIM // 2, tm), lambda i: (0, i % tiles_per_seq))
    t_spec = pl.BlockSpec((GROUP_W, tm), lambda i: (i // tiles_per_seq, i % tiles_per_seq))
    out_shapes = (
        jax.ShapeDtypeStruct((batch * GROUP_W, seq), BF16),
        jax.ShapeDtypeStruct((rows, GROUP_W), BF16),
        jax.ShapeDtypeStruct((batch * GROUP_W, seq), BF16),
        jax.ShapeDtypeStruct((rows, HEADS * ML_QK), BF16),
        jax.ShapeDtypeStruct((rows, HEADS * ML_QK), BF16),
        jax.ShapeDtypeStruct((batch * GROUP_W, seq), BF16),
        jax.ShapeDtypeStruct((rows, GROUP_W), BF16),
        jax.ShapeDtypeStruct((batch * N_GATES, seq), F32),
    )
    gate_spec = pl.BlockSpec((N_GATES, tm), lambda i: (i // tiles_per_seq, i % tiles_per_seq))
    out_specs = (t_spec, row(GROUP_W), t_spec, row(HEADS * ML_QK), row(HEADS * ML_QK),
                 t_spec, row(GROUP_W), gate_spec)
    cos, sin, cos_t, sin_t = rope
    return pl.pallas_call(
        functools.partial(_inproj_kernel, tiles_per_seq),
        grid=(rows // tm,),
        in_specs=[row(D_MODEL), _resident(gains, layer), _resident(w_rows, layer),
                  _resident(w_cols_t, layer),
                  rope_spec, rope_spec, rope_t_spec, rope_t_spec,
                  _resident(conv_w, layer), _resident(conv_b, layer)],
        out_specs=out_specs,
        out_shape=out_shapes,
        scratch_shapes=[pltpu.VMEM((tm + 2 * SUBLANES, GROUP_W), F32)],
        compiler_params=_params("arbitrary"),
        name="mixer_inproj",
    )(x, gains, w_rows, w_cols_t, cos, sin, cos_t, sin_t, conv_w, conv_b)


def _attn_kernel(lam_init, lam_ref, subg_ref, qt_ref, k_ref, vt_ref, o_ref,
                 m_sc, l_sc, acc_sc, s_sc):
    blk = ATT_BLOCK
    n_blocks = qt_ref.shape[1] // blk

    def query_block(i, carry):
        q_start = pl.multiple_of(i * blk, blk)
        qt = qt_ref[:, pl.ds(q_start, blk)]
        feat = lax.broadcasted_iota(jnp.int32, qt.shape, 0)
        zero = jnp.zeros_like(qt)
        qt_halves = (jnp.where(feat < DA_HEAD_DIM, qt, zero), jnp.where(feat >= DA_HEAD_DIM, qt, zero))

        def chunk_mask():
            key = lax.broadcasted_iota(jnp.int32, (blk, blk), 0) // CHUNK
            qry = lax.broadcasted_iota(jnp.int32, (blk, blk), 1) // CHUNK
            return key <= qry

        def key_block(j):
            start = pl.multiple_of(j * blk, blk)
            return k_ref[pl.ds(start, blk), :], vt_ref[:, pl.ds(start, blk)]

        def online_step(j, mask):
            kb, vb = key_block(j)
            for c in range(2):
                s = _dot(kb, qt_halves[c])
                if mask is not None:
                    s = jnp.where(mask, s, -jnp.inf)
                m_old = m_sc[c]
                m_new = jnp.maximum(m_old, jnp.max(s, axis=0, keepdims=True))
                alpha = jnp.exp2(m_old - m_new)
                p = jnp.exp2(s - m_new)
                l_sc[c] = alpha * l_sc[c] + jnp.sum(p, axis=0, keepdims=True)
                acc_sc[c] = alpha * acc_sc[c] + _dot(vb, p.astype(BF16))
                m_sc[c] = m_new

        def scores(j, slot, mask=None):
            kb, _ = key_block(j)
            for c in range(2):
                s = _dot(kb, qt_halves[c])
                if mask is not None:
                    s = jnp.where(mask, s, -jnp.inf)
                    m_sc[c] = jnp.max(s, axis=0, keepdims=True)
                p = jnp.exp2(s - m_sc[c])
                l_sc[c] += jnp.sum(p, axis=0, keepdims=True)
                s_sc[slot, c] = p.astype(BF16)

        def accumulate(j, slot):
            _, vb = key_block(j)
            for c in range(2):
                acc_sc[c] += _dot(vb, s_sc[slot, c])

        l_sc[...] = jnp.zeros(l_sc.shape, F32)
        acc_sc[...] = jnp.zeros(acc_sc.shape, F32)
        scores(i, 0, chunk_mask())

        def pair(t, carry):
            j = 2 * t
            scores(j, 1)
            accumulate(jnp.where(t == 0, i, j - 1), 0)
            scores(j + 1, 0)
            accumulate(j, 1)
            return carry

        lax.fori_loop(0, i // 2, pair, 0)

        @pl.when(i % 2 == 1)
        def _():
            scores(i - 1, 1)
            accumulate(jnp.where(i == 1, i, i - 2), 0)
            accumulate(i - 1, 1)

        @pl.when(i % 2 == 0)
        def _():
            accumulate(jnp.where(i == 0, i, i - 1), 0)

        in_range = jnp.all(l_sc[...] < ATT_L_LIMIT)

        @pl.when(jnp.logical_not(in_range))
        def _():
            m_sc[...] = jnp.full(m_sc.shape, -jnp.inf, F32)
            l_sc[...] = jnp.zeros(l_sc.shape, F32)
            acc_sc[...] = jnp.zeros(acc_sc.shape, F32)
            online_step(i, chunk_mask())

            def body(j, carry):
                online_step(j, None)
                return carry

            lax.fori_loop(0, i, body, 0)

        lv = lam_ref[...]
        lam = (jnp.exp(jnp.sum(lv[0:1] * lv[1:2], axis=-1, keepdims=True))
               - jnp.exp(jnp.sum(lv[2:3] * lv[3:4], axis=-1, keepdims=True)) + lam_init)
        o_t = acc_sc[0] / l_sc[0] - lam * (acc_sc[1] / l_sc[1])
        o_ref[pl.ds(q_start, blk), :] = (
            _rms(o_t.T, subg_ref[...]) * (1.0 - lam_init)).astype(BF16)
        return carry

    lax.fori_loop(0, n_blocks, query_block, 0)


def _attn_call(qt, k, vt, lam_vecs, subln_g, layer, lam_init, batch, seq):
    tq = ATT_BLOCK
    t_spec = pl.BlockSpec((LANES, seq), lambda b, h: (b * HEADS + h, 0))
    row_spec = pl.BlockSpec((seq, LANES), lambda b, h: (b, h))
    return pl.pallas_call(
        functools.partial(_attn_kernel, lam_init),
        grid=(batch, HEADS),
        in_specs=[_resident(lam_vecs, layer), _resident(subln_g, layer), t_spec, row_spec, t_spec],
        out_specs=row_spec,
        out_shape=jax.ShapeDtypeStruct(k.shape, BF16),
        scratch_shapes=[pltpu.VMEM((2, 1, tq), F32), pltpu.VMEM((2, 1, tq), F32),
                        pltpu.VMEM((2, LANES, tq), F32), pltpu.VMEM((2, 2, tq, tq), BF16)],
        compiler_params=_params("arbitrary", "arbitrary"),
        name="diff_attention",
    )(lam_vecs, subln_g, qt, k, vt)


def _split3(x):
    hi = x.astype(BF16)
    r = x - hi.astype(F32)
    mid = r.astype(BF16)
    lo = (r - mid.astype(F32)).astype(BF16)
    return hi, mid, lo


def _chunk_gates(raw, bias, tri):
    z = raw + bias
    log_f = jnp.minimum(z, 0.0) - jnp.log1p(jnp.exp(-jnp.abs(z)))
    hi, mid, lo = _split3(log_f)
    b = (_dot(hi, tri) + _dot(mid, tri) + _dot(lo, tri))[HEADS:]
    return z[0:HEADS] - b, b


def _mlstm_kernel(q_ref, k_ref, vt_ref, og_ref, gates_ref, bias_ref, normg_ref, o_ref,
                  state_ref, m_ref):
    blk = ML_BLOCK

    @pl.when(pl.program_id(1) == 0)
    def _():
        state_ref[...] = jnp.zeros(state_ref.shape, F32)
        m_ref[...] = jnp.zeros(m_ref.shape, F32)

    src = lax.broadcasted_iota(jnp.int32, (blk, blk), 0)
    dst = lax.broadcasted_iota(jnp.int32, (blk, blk), 1)
    causal = src <= dst
    lane = lax.broadcasted_iota(jnp.int32, (blk, LANES), 1)
    ones_rows = jnp.ones((HEAD_V, blk), BF16)
    tri = jnp.where(causal, 1.0, 0.0).astype(BF16)
    normg = normg_ref[...]
    bias = bias_ref[...]

    for chunk in range(ML_STEP_CHUNKS):
        rows = slice(chunk * blk, (chunk + 1) * blk)
        g_all, b_all = _chunk_gates(gates_ref[:, rows], bias, tri)
        for h in range(HEADS):
            pair = (h // 2) * LANES
            in_head = (lane // ML_QK) == (h % 2)
            q2 = q_ref[rows, pair:pair + LANES]
            k2 = k_ref[rows, pair:pair + LANES]
            kh = jnp.where(in_head, k2, jnp.zeros_like(k2))
            v_ext = jnp.concatenate([vt_ref[h * HEAD_V:(h + 1) * HEAD_V, rows], ones_rows], axis=0)

            b_row = b_all[h:h + 1]
            g_lanes = jnp.broadcast_to(g_all[h:h + 1], (LANES, blk)).T
            g_src = jnp.concatenate([g_lanes] * (blk // LANES), axis=1)
            m_prev = m_ref[h:h + 1, 0:1]

            run_max = jnp.max(jnp.where(causal, g_src, -jnp.inf), axis=0, keepdims=True)
            m_run = jnp.maximum(m_prev, run_max)
            s_t = _dot_nt(kh, q2)
            w_t = (jnp.where(causal, jnp.exp(g_src - m_run), 0.0) * s_t).astype(BF16)
            state = state_ref[h]
            nd = _dot(v_ext, w_t) + jnp.exp(m_prev - m_run) * _dot_nt(state.astype(BF16), q2)
            num = nd[0:HEAD_V]
            den = nd[HEAD_V:]
            hid = num / jnp.maximum(jnp.abs(den), jnp.exp(-(b_row + m_run)))
            hid = hid * lax.rsqrt(jnp.mean(hid * hid, axis=0, keepdims=True) + RMS_EPS)
            out = hid.T * normg * og_ref[rows, h * HEAD_V:(h + 1) * HEAD_V].astype(F32)
            o_ref[rows, h * HEAD_V:(h + 1) * HEAD_V] = out.astype(BF16)

            m_last = m_run[:, blk - 1:blk]
            kw = jnp.where(in_head, k2.astype(F32) * jnp.exp(g_lanes - m_last), 0.0)
            state_ref[h] = jnp.exp(m_prev - m_last) * state + _dot(v_ext, kw.astype(BF16))
            m_ref[h:h + 1, 0:1] = b_row[:, blk - 1:blk] + m_last


def _mlstm_call(q, k, vt, og, gates_t, gate_b, norm_g, layer, batch, seq):
    step = ML_STEP_CHUNKS * ML_BLOCK
    ns = seq // step
    row = lambda w: pl.BlockSpec((step, w), lambda b, c: (b * ns + c, 0))
    return pl.pallas_call(
        _mlstm_kernel,
        grid=(batch, ns),
        in_specs=[row(HEADS * ML_QK), row(HEADS * ML_QK),
                  pl.BlockSpec((GROUP_W, step), lambda b, c: (b, c)), row(GROUP_W),
                  pl.BlockSpec((N_GATES, step), lambda b, c: (b, c)),
                  _resident(gate_b, layer), _resident(norm_g, layer)],
        out_specs=row(GROUP_W),
        out_shape=jax.ShapeDtypeStruct(og.shape, BF16),
        scratch_shapes=[pltpu.VMEM((HEADS, 2 * HEAD_V, LANES), F32),
                        pltpu.VMEM((SUBLANES, LANES), F32)],
        compiler_params=_params("arbitrary", "arbitrary"),
        name="mlstm",
    )(q, k, vt, og, gates_t, gate_b, norm_g)


def _rope_tables(seq):
    inv_freq = ROPE_THETA ** (-jnp.arange(0, DA_HEAD_DIM, 2, dtype=F32) / DA_HEAD_DIM)
    ang = jnp.arange(seq, dtype=F32)[:, None] * inv_freq[None, :]
    cos, sin = jnp.cos(ang), jnp.sin(ang)
    reps = LANES // DA_HEAD_DIM
    cos_l = jnp.tile(jnp.concatenate([cos, cos], axis=1), (1, reps))
    sin_l = jnp.tile(jnp.concatenate([-sin, sin], axis=1), (1, reps))
    return cos_l, sin_l, cos.T, sin.T


def _mixer(x, p, layer, lam_init, rope, batch, seq):
    qt, k, vt, mq, mk, mv, mo, gates_t = _inproj_call(
        x, p["gains"], p["w_rows"], p["w_cols_t"], rope, p["conv_w"], p["conv_b"],
        layer, batch, seq)
    y_da = _attn_call(qt, k, vt, p["lam"], p["subln_g"], layer, lam_init, batch, seq)
    y_ml = _mlstm_call(mq, mk, mv, mo, gates_t, p["gate_b"], p["norm_g"], layer, batch, seq)
    return y_da, y_ml


def _prepare(norm_gains, mix_w_in, mix_w_out, da_lambda, da_subln_g, ml_conv_w, ml_conv_b,
             ml_gate_b, ml_norm_g):
    w_bf = mix_w_in.astype(BF16)
    gate_pad = jnp.zeros((DEPTH, D_MODEL, LANES - N_GATES), F32)
    return {
        "gains": norm_gains,
        "w_rows": jnp.concatenate([w_bf[:, :, GROUP_W:2 * GROUP_W],
                                   w_bf[:, :, 3 * GROUP_W:4 * GROUP_W],
                                   w_bf[:, :, 5 * GROUP_W:N_MAIN]], axis=2),
        "w_cols_t": jnp.concatenate([mix_w_in[:, :, 0:GROUP_W],
                                     mix_w_in[:, :, 2 * GROUP_W:3 * GROUP_W],
                                     mix_w_in[:, :, 4 * GROUP_W:5 * GROUP_W],
                                     mix_w_in[:, :, N_MAIN:], gate_pad],
                                    axis=2).transpose(0, 2, 1).astype(BF16),
        "w_mix_out": mix_w_out.astype(BF16),
        "lam": da_lambda,
        "subln_g": da_subln_g[:, None, :],
        "conv_w": ml_conv_w,
        "conv_b": ml_conv_b[:, None, :],
        "gate_b": ml_gate_b[:, :, None],
        "norm_g": ml_norm_g[:, None, :],
    }


def kernel(x, ffn_w_in, ffn_w_out, norm_gains, mix_w_in, mix_w_out, da_lambda, da_subln_g,
           ml_conv_w, ml_conv_b, ml_gate_b, ml_norm_g):
    batch, seq, d = x.shape
    assert d == D_MODEL and seq % FFN_ROWS == 0 and seq % (ML_BLOCK * ML_STEP_CHUNKS) == 0
    assert seq % ATT_BLOCK == 0
    rope = _rope_tables(seq)
    p = _prepare(norm_gains, mix_w_in, mix_w_out, da_lambda, da_subln_g, ml_conv_w, ml_conv_b,
                 ml_gate_b, ml_norm_g)
    w_in = ffn_w_in.astype(BF16)
    w_out = ffn_w_out.astype(BF16)
    h = x.reshape(batch * seq, d)
    for l in range(DEPTH):
        h = _ffn_call(h, norm_gains, w_in, w_out, l)
        lam_init = 0.8 - 0.6 * math.exp(-0.3 * l)
        y_da, y_ml = _mixer(h, p, l, lam_init, rope, batch, seq)
        h = _ffn_call(h, norm_gains, w_in, w_out, l, mix=(y_da, y_ml, p["w_mix_out"]))
    return h.reshape(batch, seq, d)
```

```python
import functools
import math

import jax
import jax.numpy as jnp
from jax import lax
from jax.experimental import pallas as pl
from jax.experimental.pallas import tpu as pltpu

D_MODEL = 1024
DEPTH = 2
CHUNK = 64
ROPE_THETA = 10000.0
RMS_EPS = 1e-6
D_FF = 2816
HEADS = 4
DA_HEAD_DIM = 64
HEAD_V = 128
GROUP_W = HEADS * HEAD_V
ML_QK = 64
ML_CONV = 4
N_MAIN = 6 * GROUP_W
N_GATES = 2 * HEADS
Q_SCALE = DA_HEAD_DIM ** -0.5 * math.log2(math.e)

LANES = 128
SUBLANES = 8
VMEM_LIMIT = 52 * 1024 * 1024

FFN_ROWS = 1024
PROJ_ROWS = 512
FFN_COLS = 256
ROW_SPLIT = 2
ATT_BLOCK = 512
ATT_L_LIMIT = 2.0 ** 64
ML_BLOCK = 256
ML_STEP_CHUNKS = 4

BF16 = jnp.bfloat16
F32 = jnp.float32


def _dot(a, b):
    return jnp.dot(a, b, preferred_element_type=F32)


def _dot_nt(a, b):
    return lax.dot_general(a, b, (((1,), (1,)), ((), ())), preferred_element_type=F32)


def _dot_tn(a, b):
    return lax.dot_general(a, b, (((0,), (0,)), ((), ())), preferred_element_type=F32)


def _rms(x, g):
    return x * lax.rsqrt(jnp.mean(x * x, axis=-1, keepdims=True) + RMS_EPS) * g


def _params(*sem):
    return pltpu.CompilerParams(dimension_semantics=sem, vmem_limit_bytes=VMEM_LIMIT)


def _resident(arr, *lead):
    tail = arr.shape[len(lead):]
    index = tuple(lead) + (0,) * len(tail)
    return pl.BlockSpec((None,) * len(lead) + tail, lambda *_: index,
                        pipeline_mode=pl.Buffered(1))


def _ffn_body(x, g_pre, g_post, win_ref, wout_ref):
    xn = _rms(x, g_pre).astype(BF16)
    acc = jnp.zeros(x.shape, F32)
    for c in range(D_FF // FFN_COLS):
        lo = c * FFN_COLS
        gate = _dot(xn, win_ref[:, lo:lo + FFN_COLS])
        up = _dot(xn, win_ref[:, D_FF + lo:D_FF + lo + FFN_COLS])
        act = (gate * jax.nn.sigmoid(gate) * up).astype(BF16)
        acc = acc + _dot(act, wout_ref[lo:lo + FFN_COLS, :])
    return x + 0.5 * _rms(acc, g_post)


def _ffn_kernel(x_ref, gains_ref, win_ref, wout_ref, o_ref):
    o_ref[...] = _ffn_body(x_ref[...], gains_ref[0:1], gains_ref[1:2], win_ref, wout_ref)


def _mix_ffn_kernel(x_ref, yda_ref, yml_ref, wmix_ref, gains_ref, win_ref, wout_ref, o_ref):
    h = _dot(yda_ref[...], wmix_ref[0:GROUP_W, :]) + _dot(yml_ref[...], wmix_ref[GROUP_W:, :])
    x = x_ref[...] + _rms(h, gains_ref[3:4])
    o_ref[...] = _ffn_body(x, gains_ref[4:5], gains_ref[5:6], win_ref, wout_ref)


def _ffn_call(x, gains, w_in, w_out, layer, mix=None):
    rows = x.shape[0]
    tm = FFN_ROWS
    which = 0 if mix is None else 1
    row_spec = pl.BlockSpec((tm, D_MODEL), lambda i: (i, 0))
    half_spec = pl.BlockSpec((tm, GROUP_W), lambda i: (i, 0))
    weights = [_resident(gains, layer), _resident(w_in, layer, which),
               _resident(w_out, layer, which)]
    if mix is None:
        kern, ins, specs = _ffn_kernel, (x, gains, w_in, w_out), [row_spec] + weights
    else:
        y_da, y_ml, w_mix = mix
        kern = _mix_ffn_kernel
        ins = (x, y_da, y_ml, w_mix, gains, w_in, w_out)
        specs = [row_spec, half_spec, half_spec, _resident(w_mix, layer)] + weights
    return pl.pallas_call(
        kern,
        grid=(rows // tm,),
        in_specs=specs,
        out_specs=row_spec,
        out_shape=jax.ShapeDtypeStruct(x.shape, F32),
        compiler_params=_params("arbitrary"),
        name="ffn" if mix is None else "mix_ffn",
    )(*ins)


def _rope(t, cos, sin_signed, first_half):
    swapped = jnp.where(first_half, pltpu.roll(t, LANES - 32, axis=1), pltpu.roll(t, 32, axis=1))
    return t * cos + swapped * sin_signed


def _rope_rows(t, cos, sin):
    half = DA_HEAD_DIM // 2
    a, b = t[0:half], t[half:]
    return jnp.concatenate([a * cos - b * sin, b * cos + a * sin], axis=0)


def _inproj_kernel(tiles_per_seq, x_ref, gain_ref, w_ref, wt_ref, cos_ref, sin_ref,
                   cost_ref, sint_ref, convw_ref, convb_ref,
                   qt_ref, k_ref, vt_ref, mq_ref, mk_ref, mv_ref, mo_ref, gates_ref,
                   conv_ref):
    tm = x_ref.shape[0]
    halo = SUBLANES
    sub = tm // ROW_SPLIT

    @pl.when(pl.program_id(0) % tiles_per_seq == 0)
    def _():
        conv_ref[0:halo, :] = jnp.zeros((halo, GROUP_W), F32)

    lane = lax.broadcasted_iota(jnp.int32, (sub, LANES), 1)
    first_half = (lane % 64) < 32

    for part in range(ROW_SPLIT):
        r0 = part * sub
        rows = slice(r0, r0 + sub)
        xn = _rms(x_ref[rows, :], gain_ref[2:3]).astype(BF16)

        qv_t = _dot_nt(wt_ref[...], xn)
        gates_ref[:, rows] = qv_t[3 * GROUP_W:3 * GROUP_W + N_GATES, :]
        cos_t = cost_ref[:, rows]
        sin_t = sint_ref[:, rows]
        for r in range(GROUP_W // DA_HEAD_DIM):
            lo = r * DA_HEAD_DIM
            qt_ref[lo:lo + DA_HEAD_DIM, rows] = (
                _rope_rows(qv_t[lo:lo + DA_HEAD_DIM], cos_t, sin_t) * Q_SCALE).astype(BF16)
        vt_ref[:, rows] = qv_t[GROUP_W:2 * GROUP_W, :].astype(BF16)
        mv_ref[:, rows] = qv_t[2 * GROUP_W:3 * GROUP_W, :].astype(BF16)

        cos = cos_ref[rows, :]
        sin = sin_ref[rows, :]
        kk = _dot(xn, w_ref[:, 0:GROUP_W])
        for h in range(HEADS):
            lo = h * LANES
            k_ref[rows, lo:lo + LANES] = _rope(kk[:, lo:lo + LANES], cos, sin, first_half).astype(BF16)

        base = halo + r0
        conv_ref[base:base + sub, :] = _dot(xn, w_ref[:, GROUP_W:2 * GROUP_W])
        y = convb_ref[...] + convw_ref[ML_CONV - 1:ML_CONV, :] * conv_ref[base:base + sub, :]
        for j in range(ML_CONV - 1):
            back = ML_CONV - 1 - j
            y = y + convw_ref[j:j + 1, :] * conv_ref[base - back:base - back + sub, :]
        y = y * jax.nn.sigmoid(y)
        mq_ref[rows, :] = (y[:, 0:HEADS * ML_QK] * (ML_QK ** -0.5)).astype(BF16)
        mk_ref[rows, :] = y[:, HEADS * ML_QK:].astype(BF16)

        mo_ref[rows, :] = jax.nn.sigmoid(_dot(xn, w_ref[:, 2 * GROUP_W:3 * GROUP_W])).astype(BF16)

    conv_ref[0:halo, :] = conv_ref[tm:tm + halo, :]


def _inproj_call(x, gains, w_rows, w_cols_t, rope, conv_w, conv_b, layer, batch, seq):
    rows = x.shape[0]
    tm = PROJ_ROWS
    tiles_per_seq = seq // tm
    row = lambda w: pl.BlockSpec((tm, w), lambda i: (i, 0))
    rope_spec = pl.BlockSpec((tm, LANES), lambda i: (i % tiles_per_seq, 0))
    rope_t_spec = pl.BlockSpec((DA_HEAD_DIM // 2, tm), lambda i: (0, i % tiles_per_seq))
    t_spec = pl.BlockSpec((GROUP_W, tm), lambda i: (i // tiles_per_seq, i % tiles_per_seq))
    out_shapes = (
        jax.ShapeDtypeStruct((batch * GROUP_W, seq), BF16),
        jax.ShapeDtypeStruct((rows, GROUP_W), BF16),
        jax.ShapeDtypeStruct((batch * GROUP_W, seq), BF16),
        jax.ShapeDtypeStruct((rows, HEADS * ML_QK), BF16),
        jax.ShapeDtypeStruct((rows, HEADS * ML_QK), BF16),
        jax.ShapeDtypeStruct((batch * GROUP_W, seq), BF16),
        jax.ShapeDtypeStruct((rows, GROUP_W), BF16),
        jax.ShapeDtypeStruct((batch * N_GATES, seq), F32),
    )
    gate_spec = pl.BlockSpec((N_GATES, tm), lambda i: (i // tiles_per_seq, i % tiles_per_seq))
    out_specs = (t_spec, row(GROUP_W), t_spec, row(HEADS * ML_QK), row(HEADS * ML_QK),
                 t_spec, row(GROUP_W), gate_spec)
    cos, sin, cos_t, sin_t = rope
    return pl.pallas_call(
        functools.partial(_inproj_kernel, tiles_per_seq),
        grid=(rows // tm,),
        in_specs=[row(D_MODEL), _resident(gains, layer), _resident(w_rows, layer),
                  _resident(w_cols_t, layer),
                  rope_spec, rope_spec, rope_t_spec, rope_t_spec,
                  _resident(conv_w, layer), _resident(conv_b, layer)],
        out_specs=out_specs,
        out_shape=out_shapes,
        scratch_shapes=[pltpu.VMEM((tm + 2 * SUBLANES, GROUP_W), F32)],
        compiler_params=_params("arbitrary"),
        name="mixer_inproj",
    )(x, gains, w_rows, w_cols_t, cos, sin, cos_t, sin_t, conv_w, conv_b)


def _attn_kernel(lam_init, lam_ref, subg_ref, qt_ref, k_ref, vt_ref, o_ref,
                 m_sc, l_sc, acc_sc, s_sc, l_done, acc_done):
    blk = ATT_BLOCK
    n_blocks = qt_ref.shape[1] // blk

    lv = lam_ref[...]
    lam = (jnp.exp(jnp.sum(lv[0:1] * lv[1:2], axis=-1, keepdims=True))
           - jnp.exp(jnp.sum(lv[2:3] * lv[3:4], axis=-1, keepdims=True)) + lam_init)

    def write_out(block):
        o_t = acc_done[0] / l_done[0] - lam * (acc_done[1] / l_done[1])
        start = pl.multiple_of(block * blk, blk)
        o_ref[pl.ds(start, blk), :] = (
            _rms(o_t.T, subg_ref[...]) * (1.0 - lam_init)).astype(BF16)

    l_done[...] = jnp.ones(l_done.shape, F32)
    acc_done[...] = jnp.zeros(acc_done.shape, F32)

    def query_block(i, carry):
        q_start = pl.multiple_of(i * blk, blk)
        qt = qt_ref[:, pl.ds(q_start, blk)]
        feat = lax.broadcasted_iota(jnp.int32, qt.shape, 0)
        zero = jnp.zeros_like(qt)
        qt_halves = (jnp.where(feat < DA_HEAD_DIM, qt, zero), jnp.where(feat >= DA_HEAD_DIM, qt, zero))

        def chunk_mask():
            key = lax.broadcasted_iota(jnp.int32, (blk, blk), 0) // CHUNK
            qry = lax.broadcasted_iota(jnp.int32, (blk, blk), 1) // CHUNK
            return key <= qry

        def key_block(j):
            start = pl.multiple_of(j * blk, blk)
            return k_ref[pl.ds(start, blk), :], vt_ref[:, pl.ds(start, blk)]

        def online_step(j, mask):
            kb, vb = key_block(j)
            for c in range(2):
                s = _dot(kb, qt_halves[c])
                if mask is not None:
                    s = jnp.where(mask, s, -jnp.inf)
                m_old = m_sc[c]
                m_new = jnp.maximum(m_old, jnp.max(s, axis=0, keepdims=True))
                alpha = jnp.exp2(m_old - m_new)
                p = jnp.exp2(s - m_new)
                l_sc[c] = alpha * l_sc[c] + jnp.sum(p, axis=0, keepdims=True)
                acc_sc[c] = alpha * acc_sc[c] + _dot(vb, p.astype(BF16))
                m_sc[c] = m_new

        def scores(j, slot, mask=None):
            kb, _ = key_block(j)
            for c in range(2):
                s = _dot(kb, qt_halves[c])
                if mask is not None:
                    s = jnp.where(mask, s, -jnp.inf)
                    m_sc[c] = jnp.max(s, axis=0, keepdims=True)
                p = jnp.exp2(s - m_sc[c])
                l_sc[c] += jnp.sum(p, axis=0, keepdims=True)
                s_sc[slot, c] = p.astype(BF16)

        def accumulate(j, slot):
            _, vb = key_block(j)
            for c in range(2):
                acc_sc[c] += _dot(vb, s_sc[slot, c])

        write_out(jnp.maximum(i - 1, 0))
        l_sc[...] = jnp.zeros(l_sc.shape, F32)
        acc_sc[...] = jnp.zeros(acc_sc.shape, F32)
        scores(i, 0, chunk_mask())

        def pair(t, carry):
            j = 2 * t
            scores(j, 1)
            accumulate(jnp.where(t == 0, i, j - 1), 0)
            scores(j + 1, 0)
            accumulate(j, 1)
            return carry

        lax.fori_loop(0, i // 2, pair, 0)

        @pl.when(i % 2 == 1)
        def _():
            scores(i - 1, 1)
            accumulate(jnp.where(i == 1, i, i - 2), 0)
            accumulate(i - 1, 1)

        @pl.when(i % 2 == 0)
        def _():
            accumulate(jnp.where(i == 0, i, i - 1), 0)

        in_range = jnp.all(l_sc[...] < ATT_L_LIMIT)

        @pl.when(jnp.logical_not(in_range))
        def _():
            m_sc[...] = jnp.full(m_sc.shape, -jnp.inf, F32)
            l_sc[...] = jnp.zeros(l_sc.shape, F32)
            acc_sc[...] = jnp.zeros(acc_sc.shape, F32)
            online_step(i, chunk_mask())

            def body(j, carry):
                online_step(j, None)
                return carry

            lax.fori_loop(0, i, body, 0)

        l_done[...] = l_sc[...]
        acc_done[...] = acc_sc[...]
        return carry

    lax.fori_loop(0, n_blocks, query_block, 0)
    write_out(n_blocks - 1)


def _attn_call(qt, k, vt, lam_vecs, subln_g, layer, lam_init, batch, seq):
    tq = ATT_BLOCK
    t_spec = pl.BlockSpec((LANES, seq), lambda b, h: (b * HEADS + h, 0))
    row_spec = pl.BlockSpec((seq, LANES), lambda b, h: (b, h))
    return pl.pallas_call(
        functools.partial(_attn_kernel, lam_init),
        grid=(batch, HEADS),
        in_specs=[_resident(lam_vecs, layer), _resident(subln_g, layer), t_spec, row_spec, t_spec],
        out_specs=row_spec,
        out_shape=jax.ShapeDtypeStruct(k.shape, BF16),
        scratch_shapes=[pltpu.VMEM((2, 1, tq), F32), pltpu.VMEM((2, 1, tq), F32),
                        pltpu.VMEM((2, LANES, tq), F32), pltpu.VMEM((2, 2, tq, tq), BF16),
                        pltpu.VMEM((2, 1, tq), F32), pltpu.VMEM((2, LANES, tq), F32)],
        compiler_params=_params("arbitrary", "arbitrary"),
        name="diff_attention",
    )(lam_vecs, subln_g, qt, k, vt)


def _split3(x):
    hi = x.astype(BF16)
    r = x - hi.astype(F32)
    mid = r.astype(BF16)
    lo = (r - mid.astype(F32)).astype(BF16)
    return hi, mid, lo


def _chunk_gates(raw, bias, tri):
    z = raw + bias
    log_f = jnp.minimum(z, 0.0) - jnp.log1p(jnp.exp(-jnp.abs(z)))
    hi, mid, lo = _split3(log_f)
    b = (_dot(hi, tri) + _dot(mid, tri) + _dot(lo, tri))[HEADS:]
    return z[0:HEADS] - b, b


def _mlstm_kernel(q_ref, k_ref, vt_ref, og_ref, gates_ref, bias_ref, normg_ref, o_ref,
                  state_ref, m_ref):
    blk = ML_BLOCK

    @pl.when(pl.program_id(1) == 0)
    def _():
        state_ref[...] = jnp.zeros(state_ref.shape, F32)
        m_ref[...] = jnp.zeros(m_ref.shape, F32)

    src = lax.broadcasted_iota(jnp.int32, (blk, blk), 0)
    dst = lax.broadcasted_iota(jnp.int32, (blk, blk), 1)
    causal = src <= dst
    lane = lax.broadcasted_iota(jnp.int32, (blk, LANES), 1)
    ones_rows = jnp.ones((HEAD_V, blk), BF16)
    tri = jnp.where(causal, 1.0, 0.0).astype(BF16)
    normg = normg_ref[...]
    bias = bias_ref[...]

    for chunk in range(ML_STEP_CHUNKS):
        rows = slice(chunk * blk, (chunk + 1) * blk)
        g_all, b_all = _chunk_gates(gates_ref[:, rows], bias, tri)
        for h in range(HEADS):
            pair = (h // 2) * LANES
            in_head = (lane // ML_QK) == (h % 2)
            q2 = q_ref[rows, pair:pair + LANES]
            k2 = k_ref[rows, pair:pair + LANES]
            kh = jnp.where(in_head, k2, jnp.zeros_like(k2))
            v_ext = jnp.concatenate([vt_ref[h * HEAD_V:(h + 1) * HEAD_V, rows], ones_rows], axis=0)

            b_row = b_all[h:h + 1]
            g_lanes = jnp.broadcast_to(g_all[h:h + 1], (LANES, blk)).T
            g_src = jnp.concatenate([g_lanes] * (blk // LANES), axis=1)
            m_prev = m_ref[h:h + 1, 0:1]

            run_max = jnp.max(jnp.where(causal, g_src, -jnp.inf), axis=0, keepdims=True)
            m_run = jnp.maximum(m_prev, run_max)
            s_t = _dot_nt(kh, q2)
            w_t = (jnp.where(causal, jnp.exp(g_src - m_run), 0.0) * s_t).astype(BF16)
            state = state_ref[h]
            nd = _dot(v_ext, w_t) + jnp.exp(m_prev - m_run) * _dot_nt(state.astype(BF16), q2)
            num = nd[0:HEAD_V]
            den = nd[HEAD_V:]
            hid = num / jnp.maximum(jnp.abs(den), jnp.exp(-(b_row + m_run)))
            hid = hid * lax.rsqrt(jnp.mean(hid * hid, axis=0, keepdims=True) + RMS_EPS)
            out = hid.T * normg * og_ref[rows, h * HEAD_V:(h + 1) * HEAD_V].astype(F32)
            o_ref[rows, h * HEAD_V:(h + 1) * HEAD_V] = out.astype(BF16)

            m_last = m_run[:, blk - 1:blk]
            kw = jnp.where(in_head, k2.astype(F32) * jnp.exp(g_lanes - m_last), 0.0)
            state_ref[h] = jnp.exp(m_prev - m_last) * state + _dot(v_ext, kw.astype(BF16))
            m_ref[h:h + 1, 0:1] = b_row[:, blk - 1:blk] + m_last


def _mlstm_call(q, k, vt, og, gates_t, gate_b, norm_g, layer, batch, seq):
    step = ML_STEP_CHUNKS * ML_BLOCK
    ns = seq // step
    row = lambda w: pl.BlockSpec((step, w), lambda b, c: (b * ns + c, 0))
    return pl.pallas_call(
        _mlstm_kernel,
        grid=(batch, ns),
        in_specs=[row(HEADS * ML_QK), row(HEADS * ML_QK),
                  pl.BlockSpec((GROUP_W, step), lambda b, c: (b, c)), row(GROUP_W),
                  pl.BlockSpec((N_GATES, step), lambda b, c: (b, c)),
                  _resident(gate_b, layer), _resident(norm_g, layer)],
        out_specs=row(GROUP_W),
        out_shape=jax.ShapeDtypeStruct(og.shape, BF16),
        scratch_shapes=[pltpu.VMEM((HEADS, 2 * HEAD_V, LANES), F32),
                        pltpu.VMEM((SUBLANES, LANES), F32)],
        compiler_params=_params("arbitrary", "arbitrary"),
        name="mlstm",
    )(q, k, vt, og, gates_t, gate_b, norm_g)


def _rope_tables(seq):
    inv_freq = ROPE_THETA ** (-jnp.arange(0, DA_HEAD_DIM, 2, dtype=F32) / DA_HEAD_DIM)
    ang = jnp.arange(seq, dtype=F32)[:, None] * inv_freq[None, :]
    cos, sin = jnp.cos(ang), jnp.sin(ang)
    reps = LANES // DA_HEAD_DIM
    cos_l = jnp.tile(jnp.concatenate([cos, cos], axis=1), (1, reps))
    sin_l = jnp.tile(jnp.concatenate([-sin, sin], axis=1), (1, reps))
    return cos_l, sin_l, cos.T, sin.T


def _mixer(x, p, layer, lam_init, rope, batch, seq):
    qt, k, vt, mq, mk, mv, mo, gates_t = _inproj_call(
        x, p["gains"], p["w_rows"], p["w_cols_t"], rope, p["conv_w"], p["conv_b"],
        layer, batch, seq)
    y_da = _attn_call(qt, k, vt, p["lam"], p["subln_g"], layer, lam_init, batch, seq)
    y_ml = _mlstm_call(mq, mk, mv, mo, gates_t, p["gate_b"], p["norm_g"], layer, batch, seq)
    return y_da, y_ml


def _prepare(norm_gains, mix_w_in, mix_w_out, da_lambda, da_subln_g, ml_conv_w, ml_conv_b,
             ml_gate_b, ml_norm_g):
    w_bf = mix_w_in.astype(BF16)
    gate_pad = jnp.zeros((DEPTH, D_MODEL, LANES - N_GATES), F32)
    return {
        "gains": norm_gains,
        "w_rows": jnp.concatenate([w_bf[:, :, GROUP_W:2 * GROUP_W],
                                   w_bf[:, :, 3 * GROUP_W:4 * GROUP_W],
                                   w_bf[:, :, 5 * GROUP_W:N_MAIN]], axis=2),
        "w_cols_t": jnp.concatenate([mix_w_in[:, :, 0:GROUP_W],
                                     mix_w_in[:, :, 2 * GROUP_W:3 * GROUP_W],
                                     mix_w_in[:, :, 4 * GROUP_W:5 * GROUP_W],
                                     mix_w_in[:, :, N_MAIN:], gate_pad],
                                    axis=2).transpose(0, 2, 1).astype(BF16),
        "w_mix_out": mix_w_out.astype(BF16),
        "lam": da_lambda,
        "subln_g": da_subln_g[:, None, :],
        "conv_w": ml_conv_w,
        "conv_b": ml_conv_b[:, None, :],
        "gate_b": ml_gate_b[:, :, None],
        "norm_g": ml_norm_g[:, None, :],
    }


def kernel(x, ffn_w_in, ffn_w_out, norm_gains, mix_w_in, mix_w_out, da_lambda, da_subln_g,
           ml_conv_w, ml_conv_b, ml_gate_b, ml_norm_g):
    batch, seq, d = x.shape
    assert d == D_MODEL and (batch * seq) % FFN_ROWS == 0 and seq % PROJ_ROWS == 0
    assert seq % (ML_BLOCK * ML_STEP_CHUNKS) == 0
    assert seq % ATT_BLOCK == 0
    rope = _rope_tables(seq)
    p = _prepare(norm_gains, mix_w_in, mix_w_out, da_lambda, da_subln_g, ml_conv_w, ml_conv_b,
                 ml_gate_b, ml_norm_g)
    w_in = ffn_w_in.astype(BF16)
    w_out = ffn_w_out.astype(BF16)
    h = x.reshape(batch * seq, d)
    for l in range(DEPTH):
        h = _ffn_call(h, norm_gains, w_in, w_out, l)
        lam_init = 0.8 - 0.6 * math.exp(-0.3 * l)
        y_da, y_ml = _mixer(h, p, l, lam_init, rope, batch, seq)
        h = _ffn_call(h, norm_gains, w_in, w_out, l, mix=(y_da, y_ml, p["w_mix_out"]))
    return h.reshape(batch, seq, d)
```

```python
import functools
import math

import jax
import jax.numpy as jnp
from jax import lax
from jax.experimental import pallas as pl
from jax.experimental.pallas import tpu as pltpu

D_MODEL = 1024
DEPTH = 2
CHUNK = 64
ROPE_THETA = 10000.0
RMS_EPS = 1e-6
D_FF = 2816
HEADS = 4
DA_HEAD_DIM = 64
HEAD_V = 128
GROUP_W = HEADS * HEAD_V
ML_QK = 64
ML_CONV = 4
N_MAIN = 6 * GROUP_W
N_GATES = 2 * HEADS
Q_SCALE = DA_HEAD_DIM ** -0.5 * math.log2(math.e)

LANES = 128
SUBLANES = 8
VMEM_LIMIT = 58 * 1024 * 1024

FFN_ROWS = 512
PROJ_ROWS = 512
FFN_COLS = 256
ROW_SPLIT = 2
ATT_BLOCK = 512
ATT_L_LIMIT = 2.0 ** 64
ML_BLOCK = 256
ML_STEP_CHUNKS = 4

BF16 = jnp.bfloat16
F32 = jnp.float32


def _dot(a, b):
    return jnp.dot(a, b, preferred_element_type=F32)


def _dot_nt(a, b):
    return lax.dot_general(a, b, (((1,), (1,)), ((), ())), preferred_element_type=F32)


def _dot_tn(a, b):
    return lax.dot_general(a, b, (((0,), (0,)), ((), ())), preferred_element_type=F32)


def _rms(x, g):
    return x * lax.rsqrt(jnp.mean(x * x, axis=-1, keepdims=True) + RMS_EPS) * g


def _params(*sem):
    return pltpu.CompilerParams(dimension_semantics=sem, vmem_limit_bytes=VMEM_LIMIT)


def _resident(arr, *lead):
    tail = arr.shape[len(lead):]
    index = tuple(lead) + (0,) * len(tail)
    return pl.BlockSpec((None,) * len(lead) + tail, lambda *_: index,
                        pipeline_mode=pl.Buffered(1))


def _ffn_body(x, g_pre, g_post, win_ref, wout_ref):
    xn = _rms(x, g_pre).astype(BF16)
    acc = jnp.zeros(x.shape, F32)
    for c in range(D_FF // FFN_COLS):
        lo = c * FFN_COLS
        gate = _dot(xn, win_ref[:, lo:lo + FFN_COLS].astype(BF16))
        up = _dot(xn, win_ref[:, D_FF + lo:D_FF + lo + FFN_COLS].astype(BF16))
        act = (gate * jax.nn.sigmoid(gate) * up).astype(BF16)
        acc = acc + _dot(act, wout_ref[lo:lo + FFN_COLS, :].astype(BF16))
    return x + 0.5 * _rms(acc, g_post)


def _ffn_kernel(x_ref, gains_ref, win_ref, wout_ref, o_ref):
    o_ref[...] = _ffn_body(x_ref[...], gains_ref[0:1], gains_ref[1:2], win_ref, wout_ref)


def _mix_ffn_kernel(x_ref, yda_ref, yml_ref, wmix_ref, gains_ref, win_ref, wout_ref, o_ref):
    h = _dot(yda_ref[...], wmix_ref[0:GROUP_W, :]) + _dot(yml_ref[...], wmix_ref[GROUP_W:, :])
    x = x_ref[...] + _rms(h, gains_ref[3:4])
    o_ref[...] = _ffn_body(x, gains_ref[4:5], gains_ref[5:6], win_ref, wout_ref)


def _ffn_call(x, gains, w_in, w_out, layer, mix=None):
    rows = x.shape[0]
    tm = FFN_ROWS
    which = 0 if mix is None else 1
    row_spec = pl.BlockSpec((tm, D_MODEL), lambda i: (i, 0))
    half_spec = pl.BlockSpec((tm, GROUP_W), lambda i: (i, 0))
    weights = [_resident(gains, layer), _resident(w_in, layer, which),
               _resident(w_out, layer, which)]
    if mix is None:
        kern, ins, specs = _ffn_kernel, (x, gains, w_in, w_out), [row_spec] + weights
    else:
        y_da, y_ml, w_mix = mix
        kern = _mix_ffn_kernel
        ins = (x, y_da, y_ml, w_mix, gains, w_in, w_out)
        specs = [row_spec, half_spec, half_spec, _resident(w_mix, layer)] + weights
    return pl.pallas_call(
        kern,
        grid=(rows // tm,),
        in_specs=specs,
        out_specs=row_spec,
        out_shape=jax.ShapeDtypeStruct(x.shape, F32),
        compiler_params=_params("arbitrary"),
        name="ffn" if mix is None else "mix_ffn",
    )(*ins)


def _rope(t, cos, sin_signed, first_half):
    swapped = jnp.where(first_half, pltpu.roll(t, LANES - 32, axis=1), pltpu.roll(t, 32, axis=1))
    return t * cos + swapped * sin_signed


def _rope_rows(t, cos, sin):
    half = DA_HEAD_DIM // 2
    a, b = t[0:half], t[half:]
    return jnp.concatenate([a * cos - b * sin, b * cos + a * sin], axis=0)


def _inproj_kernel(tiles_per_seq, x_ref, gain_ref, w_ref, wt_ref, cos_ref, sin_ref,
                   cost_ref, sint_ref, convw_ref, convb_ref,
                   qt_ref, k_ref, vt_ref, mq_ref, mk_ref, mv_ref, mo_ref, gates_ref,
                   conv_ref):
    tm = x_ref.shape[0]
    halo = SUBLANES
    sub = tm // ROW_SPLIT

    @pl.when(pl.program_id(0) % tiles_per_seq == 0)
    def _():
        conv_ref[0:halo, :] = jnp.zeros((halo, GROUP_W), F32)

    lane = lax.broadcasted_iota(jnp.int32, (sub, LANES), 1)
    first_half = (lane % 64) < 32

    for part in range(ROW_SPLIT):
        r0 = part * sub
        rows = slice(r0, r0 + sub)
        xn = _rms(x_ref[rows, :], gain_ref[2:3]).astype(BF16)

        qv_t = _dot_nt(wt_ref[...], xn)
        gates_ref[:, rows] = qv_t[3 * GROUP_W:3 * GROUP_W + N_GATES, :]
        cos_t = cost_ref[:, rows]
        sin_t = sint_ref[:, rows]
        for r in range(GROUP_W // DA_HEAD_DIM):
            lo = r * DA_HEAD_DIM
            qt_ref[lo:lo + DA_HEAD_DIM, rows] = (
                _rope_rows(qv_t[lo:lo + DA_HEAD_DIM], cos_t, sin_t) * Q_SCALE).astype(BF16)
        vt_ref[:, rows] = qv_t[GROUP_W:2 * GROUP_W, :].astype(BF16)
        mv_ref[:, rows] = qv_t[2 * GROUP_W:3 * GROUP_W, :].astype(BF16)

        cos = cos_ref[rows, :]
        sin = sin_ref[rows, :]
        kk = _dot(xn, w_ref[:, 0:GROUP_W])
        for h in range(HEADS):
            lo = h * LANES
            k_ref[rows, lo:lo + LANES] = _rope(kk[:, lo:lo + LANES], cos, sin, first_half).astype(BF16)

        base = halo + r0
        conv_ref[base:base + sub, :] = _dot(xn, w_ref[:, GROUP_W:2 * GROUP_W])
        y = convb_ref[...] + convw_ref[ML_CONV - 1:ML_CONV, :] * conv_ref[base:base + sub, :]
        for j in range(ML_CONV - 1):
            back = ML_CONV - 1 - j
            y = y + convw_ref[j:j + 1, :] * conv_ref[base - back:base - back + sub, :]
        y = y * jax.nn.sigmoid(y)
        mq_ref[rows, :] = (y[:, 0:HEADS * ML_QK] * (ML_QK ** -0.5)).astype(BF16)
        mk_ref[rows, :] = y[:, HEADS * ML_QK:].astype(BF16)

        mo_ref[rows, :] = jax.nn.sigmoid(_dot(xn, w_ref[:, 2 * GROUP_W:3 * GROUP_W])).astype(BF16)

    conv_ref[0:halo, :] = conv_ref[tm:tm + halo, :]


def _inproj_call(x, gains, w_rows, w_cols_t, rope, conv_w, conv_b, layer, batch, seq):
    rows = x.shape[0]
    tm = PROJ_ROWS
    tiles_per_seq = seq // tm
    row = lambda w: pl.BlockSpec((tm, w), lambda i: (i, 0))
    rope_spec = pl.BlockSpec((tm, LANES), lambda i: (i % tiles_per_seq, 0))
    rope_t_spec = pl.BlockSpec((DA_HEAD_DIM // 2, tm), lambda i: (0, i % tiles_per_seq))
    t_spec = pl.BlockSpec((GROUP_W, tm), lambda i: (i // tiles_per_seq, i % tiles_per_seq))
    out_shapes = (
        jax.ShapeDtypeStruct((batch * GROUP_W, seq), BF16),
        jax.ShapeDtypeStruct((rows, GROUP_W), BF16),
        jax.ShapeDtypeStruct((batch * GROUP_W, seq), BF16),
        jax.ShapeDtypeStruct((rows, HEADS * ML_QK), BF16),
        jax.ShapeDtypeStruct((rows, HEADS * ML_QK), BF16),
        jax.ShapeDtypeStruct((batch * GROUP_W, seq), BF16),
        jax.ShapeDtypeStruct((rows, GROUP_W), BF16),
        jax.ShapeDtypeStruct((batch * N_GATES, seq), F32),
    )
    gate_spec = pl.BlockSpec((N_GATES, tm), lambda i: (i // tiles_per_seq, i % tiles_per_seq))
    out_specs = (t_spec, row(GROUP_W), t_spec, row(HEADS * ML_QK), row(HEADS * ML_QK),
                 t_spec, row(GROUP_W), gate_spec)
    cos, sin, cos_t, sin_t = rope
    return pl.pallas_call(
        functools.partial(_inproj_kernel, tiles_per_seq),
        grid=(rows // tm,),
        in_specs=[row(D_MODEL), _resident(gains, layer), _resident(w_rows, layer),
                  _resident(w_cols_t, layer),
                  rope_spec, rope_spec, rope_t_spec, rope_t_spec,
                  _resident(conv_w, layer), _resident(conv_b, layer)],
        out_specs=out_specs,
        out_shape=out_shapes,
        scratch_shapes=[pltpu.VMEM((tm + 2 * SUBLANES, GROUP_W), F32)],
        compiler_params=_params("arbitrary"),
        name="mixer_inproj",
    )(x, gains, w_rows, w_cols_t, cos, sin, cos_t, sin_t, conv_w, conv_b)


def _attn_kernel(lam_init, lam_ref, subg_ref, qt_ref, k_ref, vt_ref, o_ref,
                 m_sc, l_sc, acc_sc, s_sc, l_done, acc_done):
    blk = ATT_BLOCK
    n_blocks = qt_ref.shape[1] // blk

    lv = lam_ref[...]
    lam = (jnp.exp(jnp.sum(lv[0:1] * lv[1:2], axis=-1, keepdims=True))
           - jnp.exp(jnp.sum(lv[2:3] * lv[3:4], axis=-1, keepdims=True)) + lam_init)

    def write_out(block):
        o_t = acc_done[0] / l_done[0] - lam * (acc_done[1] / l_done[1])
        start = pl.multiple_of(block * blk, blk)
        o_ref[pl.ds(start, blk), :] = (
            _rms(o_t.T, subg_ref[...]) * (1.0 - lam_init)).astype(BF16)

    l_done[...] = jnp.ones(l_done.shape, F32)
    acc_done[...] = jnp.zeros(acc_done.shape, F32)

    def query_block(i, carry):
        q_start = pl.multiple_of(i * blk, blk)
        qt = qt_ref[:, pl.ds(q_start, blk)]
        feat = lax.broadcasted_iota(jnp.int32, qt.shape, 0)
        zero = jnp.zeros_like(qt)
        qt_halves = (jnp.where(feat < DA_HEAD_DIM, qt, zero), jnp.where(feat >= DA_HEAD_DIM, qt, zero))

        def chunk_mask():
            key = lax.broadcasted_iota(jnp.int32, (blk, blk), 0) // CHUNK
            qry = lax.broadcasted_iota(jnp.int32, (blk, blk), 1) // CHUNK
            return key <= qry

        def key_block(j):
            start = pl.multiple_of(j * blk, blk)
            return k_ref[pl.ds(start, blk), :], vt_ref[:, pl.ds(start, blk)]

        def online_step(j, mask):
            kb, vb = key_block(j)
            for c in range(2):
                s = _dot(kb, qt_halves[c])
                if mask is not None:
                    s = jnp.where(mask, s, -jnp.inf)
                m_old = m_sc[c]
                m_new = jnp.maximum(m_old, jnp.max(s, axis=0, keepdims=True))
                alpha = jnp.exp2(m_old - m_new)
                p = jnp.exp2(s - m_new)
                l_sc[c] = alpha * l_sc[c] + jnp.sum(p, axis=0, keepdims=True)
                acc_sc[c] = alpha * acc_sc[c] + _dot(vb, p.astype(BF16))
                m_sc[c] = m_new

        def scores(j, slot, mask=None):
            kb, _ = key_block(j)
            for c in range(2):
                s = _dot(kb, qt_halves[c])
                if mask is not None:
                    s = jnp.where(mask, s, -jnp.inf)
                    m_sc[c] = jnp.max(s, axis=0, keepdims=True)
                p = jnp.exp2(s - m_sc[c])
                l_sc[c] += jnp.sum(p, axis=0, keepdims=True)
                s_sc[slot, c] = p.astype(BF16)

        def accumulate(j, slot):
            _, vb = key_block(j)
            for c in range(2):
                acc_sc[c] += _dot(vb, s_sc[slot, c])

        write_out(jnp.maximum(i - 1, 0))
        l_sc[...] = jnp.zeros(l_sc.shape, F32)
        acc_sc[...] = jnp.zeros(acc_sc.shape, F32)
        scores(i, 0, chunk_mask())

        def pair(j):
            scores(j, 1)
            accumulate(jnp.where(j == 0, i, j - 1), 0)
            scores(j + 1, 0)
            accumulate(j, 1)

        def quad(t, carry):
            pair(4 * t)
            pair(4 * t + 2)
            return carry

        lax.fori_loop(0, i // 4, quad, 0)

        @pl.when(i % 4 >= 2)
        def _():
            pair(i - i % 4)

        @pl.when(i % 2 == 1)
        def _():
            scores(i - 1, 1)
            accumulate(jnp.where(i == 1, i, i - 2), 0)
            accumulate(i - 1, 1)

        @pl.when(i % 2 == 0)
        def _():
            accumulate(jnp.where(i == 0, i, i - 1), 0)

        in_range = jnp.all(l_sc[...] < ATT_L_LIMIT)

        @pl.when(jnp.logical_not(in_range))
        def _():
            m_sc[...] = jnp.full(m_sc.shape, -jnp.inf, F32)
            l_sc[...] = jnp.zeros(l_sc.shape, F32)
            acc_sc[...] = jnp.zeros(acc_sc.shape, F32)
            online_step(i, chunk_mask())

            def body(j, carry):
                online_step(j, None)
                return carry

            lax.fori_loop(0, i, body, 0)

        l_done[...] = l_sc[...]
        acc_done[...] = acc_sc[...]
        return carry

    lax.fori_loop(0, n_blocks, query_block, 0)
    write_out(n_blocks - 1)


def _attn_call(qt, k, vt, lam_vecs, subln_g, layer, lam_init, batch, seq):
    tq = ATT_BLOCK
    t_spec = pl.BlockSpec((LANES, seq), lambda b, h: (b * HEADS + h, 0))
    row_spec = pl.BlockSpec((seq, LANES), lambda b, h: (b, h))
    return pl.pallas_call(
        functools.partial(_attn_kernel, lam_init),
        grid=(batch, HEADS),
        in_specs=[_resident(lam_vecs, layer), _resident(subln_g, layer), t_spec, row_spec, t_spec],
        out_specs=row_spec,
        out_shape=jax.ShapeDtypeStruct(k.shape, BF16),
        scratch_shapes=[pltpu.VMEM((2, 1, tq), F32), pltpu.VMEM((2, 1, tq), F32),
                        pltpu.VMEM((2, LANES, tq), F32), pltpu.VMEM((2, 2, tq, tq), BF16),
                        pltpu.VMEM((2, 1, tq), F32), pltpu.VMEM((2, LANES, tq), F32)],
        compiler_params=_params("arbitrary", "arbitrary"),
        name="diff_attention",
    )(lam_vecs, subln_g, qt, k, vt)


def _split3(x):
    hi = x.astype(BF16)
    r = x - hi.astype(F32)
    mid = r.astype(BF16)
    lo = (r - mid.astype(F32)).astype(BF16)
    return hi, mid, lo


def _chunk_gates(raw, bias, tri):
    z = raw + bias
    log_f = jnp.minimum(z, 0.0) - jnp.log1p(jnp.exp(-jnp.abs(z)))
    hi, mid, lo = _split3(log_f)
    b = (_dot(hi, tri) + _dot(mid, tri) + _dot(lo, tri))[HEADS:]
    return z[0:HEADS] - b, b


def _mlstm_kernel(q_ref, k_ref, vt_ref, og_ref, gates_ref, bias_ref, normg_ref, o_ref,
                  state_ref, m_ref):
    blk = ML_BLOCK

    @pl.when(pl.program_id(1) == 0)
    def _():
        state_ref[...] = jnp.zeros(state_ref.shape, F32)
        m_ref[...] = jnp.zeros(m_ref.shape, F32)

    src = lax.broadcasted_iota(jnp.int32, (blk, blk), 0)
    dst = lax.broadcasted_iota(jnp.int32, (blk, blk), 1)
    causal = src <= dst
    lane = lax.broadcasted_iota(jnp.int32, (blk, LANES), 1)
    ones_rows = jnp.ones((HEAD_V, blk), BF16)
    tri = jnp.where(causal, 1.0, 0.0).astype(BF16)
    normg = normg_ref[...]
    bias = bias_ref[...]

    for chunk in range(ML_STEP_CHUNKS):
        rows = slice(chunk * blk, (chunk + 1) * blk)
        g_all, b_all = _chunk_gates(gates_ref[:, rows], bias, tri)
        for h in range(HEADS):
            pair = (h // 2) * LANES
            in_head = (lane // ML_QK) == (h % 2)
            q2 = q_ref[rows, pair:pair + LANES]
            k2 = k_ref[rows, pair:pair + LANES]
            kh = jnp.where(in_head, k2, jnp.zeros_like(k2))
            v_ext = jnp.concatenate([vt_ref[h * HEAD_V:(h + 1) * HEAD_V, rows], ones_rows], axis=0)

            b_row = b_all[h:h + 1]
            g_lanes = jnp.broadcast_to(g_all[h:h + 1], (LANES, blk)).T
            g_src = jnp.concatenate([g_lanes] * (blk // LANES), axis=1)
            m_prev = m_ref[h:h + 1, 0:1]

            run_max = jnp.max(jnp.where(causal, g_src, -jnp.inf), axis=0, keepdims=True)
            m_run = jnp.maximum(m_prev, run_max)
            s_t = _dot_nt(kh, q2)
            w_t = (jnp.where(causal, jnp.exp(g_src - m_run), 0.0) * s_t).astype(BF16)
            state = state_ref[h]
            nd = _dot(v_ext, w_t) + jnp.exp(m_prev - m_run) * _dot_nt(state.astype(BF16), q2)
            num = nd[0:HEAD_V]
            den = nd[HEAD_V:]
            hid = num / jnp.maximum(jnp.abs(den), jnp.exp(-(b_row + m_run)))
            hid = hid * lax.rsqrt(jnp.mean(hid * hid, axis=0, keepdims=True) + RMS_EPS)
            out = hid.T * normg * og_ref[rows, h * HEAD_V:(h + 1) * HEAD_V].astype(F32)
            o_ref[rows, h * HEAD_V:(h + 1) * HEAD_V] = out.astype(BF16)

            m_last = m_run[:, blk - 1:blk]
            kw = jnp.where(in_head, k2.astype(F32) * jnp.exp(g_lanes - m_last), 0.0)
            state_ref[h] = jnp.exp(m_prev - m_last) * state + _dot(v_ext, kw.astype(BF16))
            m_ref[h:h + 1, 0:1] = b_row[:, blk - 1:blk] + m_last


def _mlstm_call(q, k, vt, og, gates_t, gate_b, norm_g, layer, batch, seq):
    step = ML_STEP_CHUNKS * ML_BLOCK
    ns = seq // step
    row = lambda w: pl.BlockSpec((step, w), lambda b, c: (b * ns + c, 0))
    return pl.pallas_call(
        _mlstm_kernel,
        grid=(batch, ns),
        in_specs=[row(HEADS * ML_QK), row(HEADS * ML_QK),
                  pl.BlockSpec((GROUP_W, step), lambda b, c: (b, c)), row(GROUP_W),
                  pl.BlockSpec((N_GATES, step), lambda b, c: (b, c)),
                  _resident(gate_b, layer), _resident(norm_g, layer)],
        out_specs=row(GROUP_W),
        out_shape=jax.ShapeDtypeStruct(og.shape, BF16),
        scratch_shapes=[pltpu.VMEM((HEADS, 2 * HEAD_V, LANES), F32),
                        pltpu.VMEM((SUBLANES, LANES), F32)],
        compiler_params=_params("arbitrary", "arbitrary"),
        name="mlstm",
    )(q, k, vt, og, gates_t, gate_b, norm_g)


def _rope_tables(seq):
    inv_freq = ROPE_THETA ** (-jnp.arange(0, DA_HEAD_DIM, 2, dtype=F32) / DA_HEAD_DIM)
    ang = jnp.arange(seq, dtype=F32)[:, None] * inv_freq[None, :]
    cos, sin = jnp.cos(ang), jnp.sin(ang)
    reps = LANES // DA_HEAD_DIM
    cos_l = jnp.tile(jnp.concatenate([cos, cos], axis=1), (1, reps))
    sin_l = jnp.tile(jnp.concatenate([-sin, sin], axis=1), (1, reps))
    return cos_l, sin_l, cos.T, sin.T


def _mixer(x, p, layer, lam_init, rope, batch, seq):
    qt, k, vt, mq, mk, mv, mo, gates_t = _inproj_call(
        x, p["gains"], p["w_rows"], p["w_cols_t"], rope, p["conv_w"], p["conv_b"],
        layer, batch, seq)
    y_da = _attn_call(qt, k, vt, p["lam"], p["subln_g"], layer, lam_init, batch, seq)
    y_ml = _mlstm_call(mq, mk, mv, mo, gates_t, p["gate_b"], p["norm_g"], layer, batch, seq)
    return y_da, y_ml


def _prepare(norm_gains, mix_w_in, mix_w_out, da_lambda, da_subln_g, ml_conv_w, ml_conv_b,
             ml_gate_b, ml_norm_g):
    w_bf = mix_w_in.astype(BF16)
    gate_pad = jnp.zeros((DEPTH, D_MODEL, LANES - N_GATES), F32)
    return {
        "gains": norm_gains,
        "w_rows": jnp.concatenate([w_bf[:, :, GROUP_W:2 * GROUP_W],
                                   w_bf[:, :, 3 * GROUP_W:4 * GROUP_W],
                                   w_bf[:, :, 5 * GROUP_W:N_MAIN]], axis=2),
        "w_cols_t": jnp.concatenate([mix_w_in[:, :, 0:GROUP_W],
                                     mix_w_in[:, :, 2 * GROUP_W:3 * GROUP_W],
                                     mix_w_in[:, :, 4 * GROUP_W:5 * GROUP_W],
                                     mix_w_in[:, :, N_MAIN:], gate_pad],
                                    axis=2).transpose(0, 2, 1).astype(BF16),
        "w_mix_out": mix_w_out.astype(BF16),
        "lam": da_lambda,
        "subln_g": da_subln_g[:, None, :],
        "conv_w": ml_conv_w,
        "conv_b": ml_conv_b[:, None, :],
        "gate_b": ml_gate_b[:, :, None],
        "norm_g": ml_norm_g[:, None, :],
    }


def kernel(x, ffn_w_in, ffn_w_out, norm_gains, mix_w_in, mix_w_out, da_lambda, da_subln_g,
           ml_conv_w, ml_conv_b, ml_gate_b, ml_norm_g):
    batch, seq, d = x.shape
    assert d == D_MODEL and (batch * seq) % FFN_ROWS == 0 and seq % PROJ_ROWS == 0
    assert seq % (ML_BLOCK * ML_STEP_CHUNKS) == 0
    assert seq % ATT_BLOCK == 0
    rope = _rope_tables(seq)
    p = _prepare(norm_gains, mix_w_in, mix_w_out, da_lambda, da_subln_g, ml_conv_w, ml_conv_b,
                 ml_gate_b, ml_norm_g)
    w_in = ffn_w_in
    w_out = ffn_w_out
    h = x.reshape(batch * seq, d)
    for l in range(DEPTH):
        h = _ffn_call(h, norm_gains, w_in, w_out, l)
        lam_init = 0.8 - 0.6 * math.exp(-0.3 * l)
        y_da, y_ml = _mixer(h, p, l, lam_init, rope, batch, seq)
        h = _ffn_call(h, norm_gains, w_in, w_out, l, mix=(y_da, y_ml, p["w_mix_out"]))
    return h.reshape(batch, seq, d)
```

```python
import functools
import math

import jax
import jax.numpy as jnp
from jax import lax
from jax.experimental import pallas as pl
from jax.experimental.pallas import tpu as pltpu

D_MODEL = 1024
DEPTH = 2
CHUNK = 64
ROPE_THETA = 10000.0
RMS_EPS = 1e-6
D_FF = 2816
HEADS = 4
DA_HEAD_DIM = 64
HEAD_V = 128
GROUP_W = HEADS * HEAD_V
ML_QK = 64
ML_CONV = 4
N_MAIN = 6 * GROUP_W
N_GATES = 2 * HEADS
Q_SCALE = DA_HEAD_DIM ** -0.5 * math.log2(math.e)

LANES = 128
SUBLANES = 8
BF16_ROWS = 16
VMEM_LIMIT = 58 * 1024 * 1024

FFN_ROWS = 512
PROJ_ROWS = 512
FFN_COLS = 256
ROW_SPLIT = 2
ATT_BLOCK = 512
ATT_L_LIMIT = 2.0 ** 64
ML_BLOCK = 256
ML_STEP_CHUNKS = 4

BF16 = jnp.bfloat16
F32 = jnp.float32


def _dot(a, b):
    return jnp.dot(a, b, preferred_element_type=F32)


def _dot_nt(a, b):
    return lax.dot_general(a, b, (((1,), (1,)), ((), ())), preferred_element_type=F32)


def _rms(x, g):
    return x * lax.rsqrt(jnp.mean(x * x, axis=-1, keepdims=True) + RMS_EPS) * g


def _params(*sem):
    return pltpu.CompilerParams(dimension_semantics=sem, vmem_limit_bytes=VMEM_LIMIT)


def _resident(arr, *lead):
    tail = arr.shape[len(lead):]
    index = tuple(lead) + (0,) * len(tail)
    return pl.BlockSpec((None,) * len(lead) + tail, lambda *_: index,
                        pipeline_mode=pl.Buffered(1))


def _ffn_body(x, g_pre, g_post, win_ref, wout_ref):
    xn = _rms(x, g_pre).astype(BF16)
    acc = jnp.zeros(x.shape, F32)
    for c in range(D_FF // FFN_COLS):
        lo = c * FFN_COLS
        gate = _dot(xn, win_ref[:, lo:lo + FFN_COLS].astype(BF16))
        up = _dot(xn, win_ref[:, D_FF + lo:D_FF + lo + FFN_COLS].astype(BF16))
        act = (gate * jax.nn.sigmoid(gate) * up).astype(BF16)
        acc = acc + _dot(act, wout_ref[lo:lo + FFN_COLS, :].astype(BF16))
    return x + 0.5 * _rms(acc, g_post)


def _ffn_kernel(x_ref, gains_ref, win_ref, wout_ref, o_ref):
    o_ref[...] = _ffn_body(x_ref[...], gains_ref[0:1], gains_ref[1:2], win_ref, wout_ref)


def _mix_ffn_kernel(x_ref, yda_ref, yml_ref, wmix_ref, gains_ref, win_ref, wout_ref, o_ref):
    h = _dot(yda_ref[...], wmix_ref[0:GROUP_W, :]) + _dot(yml_ref[...], wmix_ref[GROUP_W:, :])
    x = x_ref[...] + _rms(h, gains_ref[3:4])
    o_ref[...] = _ffn_body(x, gains_ref[4:5], gains_ref[5:6], win_ref, wout_ref)


def _ffn_call(x, gains, w_in, w_out, layer, mix=None):
    rows = x.shape[0]
    tm = FFN_ROWS
    which = 0 if mix is None else 1
    row_spec = pl.BlockSpec((tm, D_MODEL), lambda i: (i, 0))
    half_spec = pl.BlockSpec((tm, GROUP_W), lambda i: (i, 0))
    weights = [_resident(gains, layer), _resident(w_in, layer, which),
               _resident(w_out, layer, which)]
    if mix is None:
        kern, ins, specs = _ffn_kernel, (x, gains, w_in, w_out), [row_spec] + weights
    else:
        y_da, y_ml, w_mix = mix
        kern = _mix_ffn_kernel
        ins = (x, y_da, y_ml, w_mix, gains, w_in, w_out)
        specs = [row_spec, half_spec, half_spec, _resident(w_mix, layer)] + weights
    return pl.pallas_call(
        kern,
        grid=(rows // tm,),
        in_specs=specs,
        out_specs=row_spec,
        out_shape=jax.ShapeDtypeStruct(x.shape, F32),
        compiler_params=_params("arbitrary"),
        name="ffn" if mix is None else "mix_ffn",
    )(*ins)


def _rope(t, cos, sin_signed, first_half):
    swapped = jnp.where(first_half, pltpu.roll(t, LANES - 32, axis=1), pltpu.roll(t, 32, axis=1))
    return t * cos + swapped * sin_signed


def _rope_rows(t, cos, sin):
    half = DA_HEAD_DIM // 2
    a, b = t[0:half], t[half:]
    return jnp.concatenate([a * cos - b * sin, b * cos + a * sin], axis=0)


def _inproj_kernel(tiles_per_seq, x_ref, gain_ref, w_ref, wt_ref, cos_ref, sin_ref,
                   cost_ref, sint_ref, convw_ref, convb_ref,
                   qt_ref, k_ref, vt_ref, mq_ref, mk_ref, mv_ref, mo_ref, gates_ref,
                   conv_ref):
    tm = x_ref.shape[0]
    halo = SUBLANES
    sub = tm // ROW_SPLIT

    @pl.when(pl.program_id(0) % tiles_per_seq == 0)
    def _():
        conv_ref[0:halo, :] = jnp.zeros((halo, GROUP_W), F32)

    lane = lax.broadcasted_iota(jnp.int32, (sub, LANES), 1)
    first_half = (lane % 64) < 32

    for part in range(ROW_SPLIT):
        r0 = part * sub
        rows = slice(r0, r0 + sub)
        xn = _rms(x_ref[rows, :], gain_ref[2:3]).astype(BF16)

        qv_t = _dot_nt(wt_ref[...], xn)
        gates_ref[:, rows] = qv_t[3 * GROUP_W:3 * GROUP_W + N_GATES, :]
        cos_t = cost_ref[:, rows]
        sin_t = sint_ref[:, rows]
        for r in range(GROUP_W // DA_HEAD_DIM):
            lo = r * DA_HEAD_DIM
            qt_ref[lo:lo + DA_HEAD_DIM, rows] = (
                _rope_rows(qv_t[lo:lo + DA_HEAD_DIM], cos_t, sin_t) * Q_SCALE).astype(BF16)
        vt_ref[:, rows] = qv_t[GROUP_W:2 * GROUP_W, :].astype(BF16)
        mv_ref[:, rows] = qv_t[2 * GROUP_W:3 * GROUP_W, :].astype(BF16)

        cos = cos_ref[rows, :]
        sin = sin_ref[rows, :]
        kk = _dot(xn, w_ref[:, GROUP_W:2 * GROUP_W])
        for h in range(HEADS):
            lo = h * LANES
            k_ref[rows, lo:lo + LANES] = _rope(kk[:, lo:lo + LANES], cos, sin, first_half).astype(BF16)

        base = halo + r0
        conv_ref[base:base + sub, :] = _dot(xn, w_ref[:, 3 * GROUP_W:4 * GROUP_W])
        y = convb_ref[...] + convw_ref[ML_CONV - 1:ML_CONV, :] * conv_ref[base:base + sub, :]
        for j in range(ML_CONV - 1):
            back = ML_CONV - 1 - j
            y = y + convw_ref[j:j + 1, :] * conv_ref[base - back:base - back + sub, :]
        y = y * jax.nn.sigmoid(y)
        mq_ref[rows, :] = (y[:, 0:HEADS * ML_QK] * (ML_QK ** -0.5)).astype(BF16)
        mk_ref[rows, :] = y[:, HEADS * ML_QK:].astype(BF16)

        mo_ref[rows, :] = jax.nn.sigmoid(
            _dot(xn, w_ref[:, 5 * GROUP_W:6 * GROUP_W])).astype(BF16)

    conv_ref[0:halo, :] = conv_ref[tm:tm + halo, :]


def _inproj_call(x, gains, w_all, w_cols_t, rope, conv_w, conv_b, layer, batch, seq):
    rows = x.shape[0]
    tm = PROJ_ROWS
    tiles_per_seq = seq // tm
    row = lambda w: pl.BlockSpec((tm, w), lambda i: (i, 0))
    rope_spec = pl.BlockSpec((tm, LANES), lambda i: (i % tiles_per_seq, 0))
    rope_t_spec = pl.BlockSpec((DA_HEAD_DIM // 2, tm), lambda i: (0, i % tiles_per_seq))
    t_spec = pl.BlockSpec((GROUP_W, tm), lambda i: (i // tiles_per_seq, i % tiles_per_seq))
    out_shapes = (
        jax.ShapeDtypeStruct((batch * GROUP_W, seq), BF16),
        jax.ShapeDtypeStruct((rows, GROUP_W), BF16),
        jax.ShapeDtypeStruct((batch * GROUP_W, seq), BF16),
        jax.ShapeDtypeStruct((rows, HEADS * ML_QK), BF16),
        jax.ShapeDtypeStruct((rows, HEADS * ML_QK), BF16),
        jax.ShapeDtypeStruct((batch * GROUP_W, seq), BF16),
        jax.ShapeDtypeStruct((rows, GROUP_W), BF16),
        jax.ShapeDtypeStruct((batch * N_GATES, seq), F32),
    )
    gate_spec = pl.BlockSpec((N_GATES, tm), lambda i: (i // tiles_per_seq, i % tiles_per_seq))
    out_specs = (t_spec, row(GROUP_W), t_spec, row(HEADS * ML_QK), row(HEADS * ML_QK),
                 t_spec, row(GROUP_W), gate_spec)
    cos, sin, cos_t, sin_t = rope
    return pl.pallas_call(
        functools.partial(_inproj_kernel, tiles_per_seq),
        grid=(rows // tm,),
        in_specs=[row(D_MODEL), _resident(gains, layer), _resident(w_all, layer),
                  _resident(w_cols_t, layer),
                  rope_spec, rope_spec, rope_t_spec, rope_t_spec,
                  _resident(conv_w, layer), _resident(conv_b, layer)],
        out_specs=out_specs,
        out_shape=out_shapes,
        scratch_shapes=[pltpu.VMEM((tm + 2 * SUBLANES, GROUP_W), F32)],
        compiler_params=_params("arbitrary"),
        name="mixer_inproj",
    )(x, gains, w_all, w_cols_t, cos, sin, cos_t, sin_t, conv_w, conv_b)


def _attn_kernel(lam_init, lam_ref, subg_ref, qt_ref, k_ref, vt_ref, o_ref,
                 m_sc, l_sc, acc_sc, s_sc, l_done, acc_done):
    blk = ATT_BLOCK
    n_blocks = qt_ref.shape[1] // blk

    lv = lam_ref[...]
    lam = (jnp.exp(jnp.sum(lv[0:1] * lv[1:2], axis=-1, keepdims=True))
           - jnp.exp(jnp.sum(lv[2:3] * lv[3:4], axis=-1, keepdims=True)) + lam_init)

    def write_out(block):
        o_t = acc_done[0] / l_done[0] - lam * (acc_done[1] / l_done[1])
        start = pl.multiple_of(block * blk, blk)
        o_ref[pl.ds(start, blk), :] = (
            _rms(o_t.T, subg_ref[...]) * (1.0 - lam_init)).astype(BF16)

    l_done[...] = jnp.ones(l_done.shape, F32)
    acc_done[...] = jnp.zeros(acc_done.shape, F32)

    def query_block(i, carry):
        q_start = pl.multiple_of(i * blk, blk)
        qt = qt_ref[:, pl.ds(q_start, blk)]
        feat = lax.broadcasted_iota(jnp.int32, qt.shape, 0)
        zero = jnp.zeros_like(qt)
        qt_halves = (jnp.where(feat < DA_HEAD_DIM, qt, zero), jnp.where(feat >= DA_HEAD_DIM, qt, zero))

        def chunk_mask():
            key = lax.broadcasted_iota(jnp.int32, (blk, blk), 0) // CHUNK
            qry = lax.broadcasted_iota(jnp.int32, (blk, blk), 1) // CHUNK
            return key <= qry

        def key_block(j):
            start = pl.multiple_of(j * blk, blk)
            return k_ref[pl.ds(start, blk), :], vt_ref[:, pl.ds(start, blk)]

        def online_step(j, mask):
            kb, vb = key_block(j)
            for c in range(2):
                s = _dot(kb, qt_halves[c])
                if mask is not None:
                    s = jnp.where(mask, s, -jnp.inf)
                m_old = m_sc[c]
                m_new = jnp.maximum(m_old, jnp.max(s, axis=0, keepdims=True))
                alpha = jnp.exp2(m_old - m_new)
                p = jnp.exp2(s - m_new)
                l_sc[c] = alpha * l_sc[c] + jnp.sum(p, axis=0, keepdims=True)
                acc_sc[c] = alpha * acc_sc[c] + _dot(vb, p.astype(BF16))
                m_sc[c] = m_new

        def scores(j, slot, mask=None):
            kb, _ = key_block(j)
            for c in range(2):
                s = _dot(kb, qt_halves[c])
                if mask is not None:
                    s = jnp.where(mask, s, -jnp.inf)
                    m_sc[c] = jnp.max(s, axis=0, keepdims=True)
                p = jnp.exp2(s - m_sc[c])
                l_sc[c] += jnp.sum(p, axis=0, keepdims=True)
                s_sc[slot, c] = p.astype(BF16)

        def accumulate(j, slot):
            _, vb = key_block(j)
            for c in range(2):
                acc_sc[c] += _dot(vb, s_sc[slot, c])

        write_out(jnp.maximum(i - 1, 0))
        l_sc[...] = jnp.zeros(l_sc.shape, F32)
        acc_sc[...] = jnp.zeros(acc_sc.shape, F32)
        scores(i, 0, chunk_mask())

        def pair(j):
            scores(j, 1)
            accumulate(jnp.where(j == 0, i, j - 1), 0)
            scores(j + 1, 0)
            accumulate(j, 1)

        def quad(t, carry):
            pair(4 * t)
            pair(4 * t + 2)
            return carry

        lax.fori_loop(0, i // 4, quad, 0)

        @pl.when(i % 4 >= 2)
        def _():
            pair(i - i % 4)

        @pl.when(i % 2 == 1)
        def _():
            scores(i - 1, 1)
            accumulate(jnp.where(i == 1, i, i - 2), 0)
            accumulate(i - 1, 1)

        @pl.when(i % 2 == 0)
        def _():
            accumulate(jnp.where(i == 0, i, i - 1), 0)

        in_range = jnp.all(l_sc[...] < ATT_L_LIMIT)

        @pl.when(jnp.logical_not(in_range))
        def _():
            m_sc[...] = jnp.full(m_sc.shape, -jnp.inf, F32)
            l_sc[...] = jnp.zeros(l_sc.shape, F32)
            acc_sc[...] = jnp.zeros(acc_sc.shape, F32)
            online_step(i, chunk_mask())

            def body(j, carry):
                online_step(j, None)
                return carry

            lax.fori_loop(0, i, body, 0)

        l_done[...] = l_sc[...]
        acc_done[...] = acc_sc[...]
        return carry

    lax.fori_loop(0, n_blocks, query_block, 0)
    write_out(n_blocks - 1)


def _attn_call(qt, k, vt, lam_vecs, subln_g, layer, lam_init, batch, seq):
    tq = ATT_BLOCK
    t_spec = pl.BlockSpec((LANES, seq), lambda b, h: (b * HEADS + h, 0))
    row_spec = pl.BlockSpec((seq, LANES), lambda b, h: (b, h))
    return pl.pallas_call(
        functools.partial(_attn_kernel, lam_init),
        grid=(batch, HEADS),
        in_specs=[_resident(lam_vecs, layer), _resident(subln_g, layer), t_spec, row_spec, t_spec],
        out_specs=row_spec,
        out_shape=jax.ShapeDtypeStruct(k.shape, BF16),
        scratch_shapes=[pltpu.VMEM((2, 1, tq), F32), pltpu.VMEM((2, 1, tq), F32),
                        pltpu.VMEM((2, LANES, tq), F32), pltpu.VMEM((2, 2, tq, tq), BF16),
                        pltpu.VMEM((2, 1, tq), F32), pltpu.VMEM((2, LANES, tq), F32)],
        compiler_params=_params("arbitrary", "arbitrary"),
        name="diff_attention",
    )(lam_vecs, subln_g, qt, k, vt)


def _split3(x):
    hi = x.astype(BF16)
    r = x - hi.astype(F32)
    mid = r.astype(BF16)
    lo = (r - mid.astype(F32)).astype(BF16)
    return hi, mid, lo


def _chunk_gates(raw, bias, tri):
    z = raw + bias
    log_f = jnp.minimum(z, 0.0) - jnp.log1p(jnp.exp(-jnp.abs(z)))
    hi, mid, lo = _split3(log_f)
    b = (_dot(hi, tri) + _dot(mid, tri) + _dot(lo, tri))[HEADS:]
    return z[0:HEADS] - b, b


def _mlstm_kernel(q_ref, k_ref, vt_ref, og_ref, gates_ref, bias_ref, normg_ref, o_ref,
                  state_ref, m_ref):
    blk = ML_BLOCK

    @pl.when(pl.program_id(1) == 0)
    def _():
        state_ref[...] = jnp.zeros(state_ref.shape, F32)
        m_ref[...] = jnp.zeros(m_ref.shape, F32)

    src = lax.broadcasted_iota(jnp.int32, (blk, blk), 0)
    dst = lax.broadcasted_iota(jnp.int32, (blk, blk), 1)
    causal = src <= dst
    lane = lax.broadcasted_iota(jnp.int32, (blk, LANES), 1)
    ones_rows = jnp.ones((HEAD_V, blk), BF16)
    tri = jnp.where(causal, 1.0, 0.0).astype(BF16)
    normg = normg_ref[...]
    bias = bias_ref[...]

    for chunk in range(ML_STEP_CHUNKS):
        rows = slice(chunk * blk, (chunk + 1) * blk)
        g_all, b_all = _chunk_gates(gates_ref[:, rows], bias, tri)
        for h in range(HEADS):
            pair = (h // 2) * LANES
            in_head = (lane // ML_QK) == (h % 2)
            q2 = q_ref[rows, pair:pair + LANES]
            k2 = k_ref[rows, pair:pair + LANES]
            kh = jnp.where(in_head, k2, jnp.zeros_like(k2))
            v_ext = jnp.concatenate([vt_ref[h * HEAD_V:(h + 1) * HEAD_V, rows], ones_rows], axis=0)

            b_row = b_all[h:h + 1]
            g_lanes = jnp.broadcast_to(g_all[h:h + 1], (LANES, blk)).T
            g_src = jnp.concatenate([g_lanes] * (blk // LANES), axis=1)
            m_prev = m_ref[h:h + 1, 0:1]

            run_max = jnp.max(jnp.where(causal, g_src, -jnp.inf), axis=0, keepdims=True)
            m_run = jnp.maximum(m_prev, run_max)
            s_t = _dot_nt(kh, q2)
            w_t = (jnp.where(causal, jnp.exp(g_src - m_run), 0.0) * s_t).astype(BF16)
            state = state_ref[h]
            nd = _dot(v_ext, w_t) + jnp.exp(m_prev - m_run) * _dot_nt(state.astype(BF16), q2)
            num = nd[0:HEAD_V]
            den = nd[HEAD_V:]
            hid = num / jnp.maximum(jnp.abs(den), jnp.exp(-(b_row + m_run)))
            hid = hid * lax.rsqrt(jnp.mean(hid * hid, axis=0, keepdims=True) + RMS_EPS)
            out = hid.T * normg * og_ref[rows, h * HEAD_V:(h + 1) * HEAD_V].astype(F32)
            o_ref[rows, h * HEAD_V:(h + 1) * HEAD_V] = out.astype(BF16)

            m_last = m_run[:, blk - 1:blk]
            kw = jnp.where(in_head, k2.astype(F32) * jnp.exp(g_lanes - m_last), 0.0)
            state_ref[h] = jnp.exp(m_prev - m_last) * state + _dot(v_ext, kw.astype(BF16))
            m_ref[h:h + 1, 0:1] = b_row[:, blk - 1:blk] + m_last


def _mlstm_call(q, k, vt, og, gates_t, gate_b, norm_g, layer, batch, seq):
    step = ML_STEP_CHUNKS * ML_BLOCK
    ns = seq // step
    row = lambda w: pl.BlockSpec((step, w), lambda b, c: (b * ns + c, 0))
    return pl.pallas_call(
        _mlstm_kernel,
        grid=(batch, ns),
        in_specs=[row(HEADS * ML_QK), row(HEADS * ML_QK),
                  pl.BlockSpec((GROUP_W, step), lambda b, c: (b, c)), row(GROUP_W),
                  pl.BlockSpec((N_GATES, step), lambda b, c: (b, c)),
                  _resident(gate_b, layer), _resident(norm_g, layer)],
        out_specs=row(GROUP_W),
        out_shape=jax.ShapeDtypeStruct(og.shape, BF16),
        scratch_shapes=[pltpu.VMEM((HEADS, 2 * HEAD_V, LANES), F32),
                        pltpu.VMEM((SUBLANES, LANES), F32)],
        compiler_params=_params("arbitrary", "arbitrary"),
        name="mlstm",
    )(q, k, vt, og, gates_t, gate_b, norm_g)


def _rope_tables(seq):
    inv_freq = ROPE_THETA ** (-jnp.arange(0, DA_HEAD_DIM, 2, dtype=F32) / DA_HEAD_DIM)
    ang = jnp.arange(seq, dtype=F32)[:, None] * inv_freq[None, :]
    cos, sin = jnp.cos(ang), jnp.sin(ang)
    reps = LANES // DA_HEAD_DIM
    cos_l = jnp.tile(jnp.concatenate([cos, cos], axis=1), (1, reps))
    sin_l = jnp.tile(jnp.concatenate([-sin, sin], axis=1), (1, reps))
    return cos_l, sin_l, cos.T, sin.T


def _mixer(x, p, layer, lam_init, rope, batch, seq):
    qt, k, vt, mq, mk, mv, mo, gates_t = _inproj_call(
        x, p["gains"], p["w_all"], p["w_cols_t"], rope, p["conv_w"], p["conv_b"],
        layer, batch, seq)
    y_da = _attn_call(qt, k, vt, p["lam"], p["subln_g"], layer, lam_init, batch, seq)
    y_ml = _mlstm_call(mq, mk, mv, mo, gates_t, p["gate_b"], p["norm_g"], layer, batch, seq)
    return y_da, y_ml


def _prepare(norm_gains, mix_w_in, mix_w_out, da_lambda, da_subln_g, ml_conv_w, ml_conv_b,
             ml_gate_b, ml_norm_g):
    gate_pad = jnp.zeros((DEPTH, D_MODEL, BF16_ROWS - N_GATES), F32)
    return {
        "gains": norm_gains,
        "w_all": mix_w_in.astype(BF16),
        "w_cols_t": jnp.concatenate([mix_w_in[:, :, 0:GROUP_W],
                                     mix_w_in[:, :, 2 * GROUP_W:3 * GROUP_W],
                                     mix_w_in[:, :, 4 * GROUP_W:5 * GROUP_W],
                                     mix_w_in[:, :, N_MAIN:], gate_pad],
                                    axis=2).transpose(0, 2, 1).astype(BF16),
        "w_mix_out": mix_w_out.astype(BF16),
        "lam": da_lambda,
        "subln_g": da_subln_g[:, None, :],
        "conv_w": ml_conv_w,
        "conv_b": ml_conv_b[:, None, :],
        "gate_b": ml_gate_b[:, :, None],
        "norm_g": ml_norm_g[:, None, :],
    }


def kernel(x, ffn_w_in, ffn_w_out, norm_gains, mix_w_in, mix_w_out, da_lambda, da_subln_g,
           ml_conv_w, ml_conv_b, ml_gate_b, ml_norm_g):
    batch, seq, d = x.shape
    assert d == D_MODEL and (batch * seq) % FFN_ROWS == 0 and seq % PROJ_ROWS == 0
    assert seq % (ML_BLOCK * ML_STEP_CHUNKS) == 0
    assert seq % ATT_BLOCK == 0
    rope = _rope_tables(seq)
    p = _prepare(norm_gains, mix_w_in, mix_w_out, da_lambda, da_subln_g, ml_conv_w, ml_conv_b,
                 ml_gate_b, ml_norm_g)
    h = x.reshape(batch * seq, d)
    for l in range(DEPTH):
        h = _ffn_call(h, norm_gains, ffn_w_in, ffn_w_out, l)
        lam_init = 0.8 - 0.6 * math.exp(-0.3 * l)
        y_da, y_ml = _mixer(h, p, l, lam_init, rope, batch, seq)
        h = _ffn_call(h, norm_gains, ffn_w_in, ffn_w_out, l, mix=(y_da, y_ml, p["w_mix_out"]))
    return h.reshape(batch, seq, d)
```

```python
import functools
import math

import jax
import jax.numpy as jnp
from jax import lax
from jax.experimental import pallas as pl
from jax.experimental.pallas import tpu as pltpu

D_MODEL = 1024
DEPTH = 2
CHUNK = 64
ROPE_THETA = 10000.0
RMS_EPS = 1e-6
D_FF = 2816
HEADS = 4
DA_HEAD_DIM = 64
HEAD_V = 128
GROUP_W = HEADS * HEAD_V
ML_QK = 64
ML_CONV = 4
N_MAIN = 6 * GROUP_W
N_GATES = 2 * HEADS
Q_SCALE = DA_HEAD_DIM ** -0.5 * math.log2(math.e)

LANES = 128
SUBLANES = 8
BF16_ROWS = 16
VMEM_LIMIT = 58 * 1024 * 1024

FFN_ROWS = 512
PROJ_ROWS = 512
FFN_COLS = 256
ROW_SPLIT = 2
ATT_BLOCK = 512
ATT_L_LIMIT = 2.0 ** 64
ML_BLOCK = 256
ML_STEP_CHUNKS = 4

BF16 = jnp.bfloat16
F32 = jnp.float32


def _dot(a, b):
    return jnp.dot(a, b, preferred_element_type=F32)


def _dot_nt(a, b):
    return lax.dot_general(a, b, (((1,), (1,)), ((), ())), preferred_element_type=F32)


def _rms(x, g):
    return x * lax.rsqrt(jnp.mean(x * x, axis=-1, keepdims=True) + RMS_EPS) * g


def _params(*sem):
    return pltpu.CompilerParams(dimension_semantics=sem, vmem_limit_bytes=VMEM_LIMIT)


def _resident(arr, *lead):
    tail = arr.shape[len(lead):]
    index = tuple(lead) + (0,) * len(tail)
    return pl.BlockSpec((None,) * len(lead) + tail, lambda *_: index,
                        pipeline_mode=pl.Buffered(1))


def _ffn_body(x, g_pre, g_post, win_ref, wout_ref):
    xn = _rms(x, g_pre).astype(BF16)
    acc = jnp.zeros(x.shape, F32)
    for c in range(D_FF // FFN_COLS):
        lo = c * FFN_COLS
        gate = _dot(xn, win_ref[:, lo:lo + FFN_COLS].astype(BF16))
        up = _dot(xn, win_ref[:, D_FF + lo:D_FF + lo + FFN_COLS].astype(BF16))
        act = (gate * jax.nn.sigmoid(gate) * up).astype(BF16)
        acc = acc + _dot(act, wout_ref[lo:lo + FFN_COLS, :].astype(BF16))
    return x + 0.5 * _rms(acc, g_post)


def _ffn_kernel(x_ref, gains_ref, win_ref, wout_ref, o_ref):
    o_ref[...] = _ffn_body(x_ref[...], gains_ref[0:1], gains_ref[1:2], win_ref, wout_ref)


def _mix_ffn_kernel(x_ref, yda_ref, yml_ref, wmix_ref, gains_ref, win_ref, wout_ref, o_ref):
    h = (_dot(yda_ref[...], wmix_ref[0:GROUP_W, :].astype(BF16))
         + _dot(yml_ref[...], wmix_ref[GROUP_W:, :].astype(BF16)))
    x = x_ref[...] + _rms(h, gains_ref[3:4])
    o_ref[...] = _ffn_body(x, gains_ref[4:5], gains_ref[5:6], win_ref, wout_ref)


def _ffn_call(x, gains, w_in, w_out, layer, mix=None):
    rows = x.shape[0]
    tm = FFN_ROWS
    which = 0 if mix is None else 1
    row_spec = pl.BlockSpec((tm, D_MODEL), lambda i: (i, 0))
    half_spec = pl.BlockSpec((tm, GROUP_W), lambda i: (i, 0))
    weights = [_resident(gains, layer), _resident(w_in, layer, which),
               _resident(w_out, layer, which)]
    if mix is None:
        kern, ins, specs = _ffn_kernel, (x, gains, w_in, w_out), [row_spec] + weights
    else:
        y_da, y_ml, w_mix = mix
        kern = _mix_ffn_kernel
        ins = (x, y_da, y_ml, w_mix, gains, w_in, w_out)
        specs = [row_spec, half_spec, half_spec, _resident(w_mix, layer)] + weights
    return pl.pallas_call(
        kern,
        grid=(rows // tm,),
        in_specs=specs,
        out_specs=row_spec,
        out_shape=jax.ShapeDtypeStruct(x.shape, F32),
        compiler_params=_params("arbitrary"),
        name="ffn" if mix is None else "mix_ffn",
    )(*ins)


def _rope(t, cos, sin_signed, first_half):
    swapped = jnp.where(first_half, pltpu.roll(t, LANES - 32, axis=1), pltpu.roll(t, 32, axis=1))
    return t * cos + swapped * sin_signed


def _rope_rows(t, cos, sin):
    half = DA_HEAD_DIM // 2
    a, b = t[0:half], t[half:]
    return jnp.concatenate([a * cos - b * sin, b * cos + a * sin], axis=0)


def _inproj_kernel(tiles_per_seq, x_ref, gain_ref, w_ref, wt_ref, cos_ref, sin_ref,
                   cost_ref, sint_ref, convw_ref, convb_ref,
                   qt_ref, k_ref, vt_ref, mq_ref, mk_ref, mv_ref, mo_ref, gates_ref,
                   conv_ref):
    tm = x_ref.shape[0]
    halo = SUBLANES
    sub = tm // ROW_SPLIT

    @pl.when(pl.program_id(0) % tiles_per_seq == 0)
    def _():
        conv_ref[0:halo, :] = jnp.zeros((halo, GROUP_W), F32)

    lane = lax.broadcasted_iota(jnp.int32, (sub, LANES), 1)
    first_half = (lane % 64) < 32

    for part in range(ROW_SPLIT):
        r0 = part * sub
        rows = slice(r0, r0 + sub)
        xn = _rms(x_ref[rows, :], gain_ref[2:3]).astype(BF16)

        qv_t = _dot_nt(wt_ref[...], xn)
        gates_ref[:, rows] = qv_t[3 * GROUP_W:3 * GROUP_W + N_GATES, :]
        cos_t = cost_ref[:, rows]
        sin_t = sint_ref[:, rows]
        for r in range(GROUP_W // DA_HEAD_DIM):
            lo = r * DA_HEAD_DIM
            qt_ref[lo:lo + DA_HEAD_DIM, rows] = (
                _rope_rows(qv_t[lo:lo + DA_HEAD_DIM], cos_t, sin_t) * Q_SCALE).astype(BF16)
        vt_ref[:, rows] = qv_t[GROUP_W:2 * GROUP_W, :].astype(BF16)
        mv_ref[:, rows] = qv_t[2 * GROUP_W:3 * GROUP_W, :].astype(BF16)

        cos = cos_ref[rows, :]
        sin = sin_ref[rows, :]
        kk = _dot(xn, w_ref[:, GROUP_W:2 * GROUP_W])
        for h in range(HEADS):
            lo = h * LANES
            k_ref[rows, lo:lo + LANES] = _rope(kk[:, lo:lo + LANES], cos, sin, first_half).astype(BF16)

        base = halo + r0
        conv_ref[base:base + sub, :] = _dot(xn, w_ref[:, 3 * GROUP_W:4 * GROUP_W])
        y = convb_ref[...] + convw_ref[ML_CONV - 1:ML_CONV, :] * conv_ref[base:base + sub, :]
        for j in range(ML_CONV - 1):
            back = ML_CONV - 1 - j
            y = y + convw_ref[j:j + 1, :] * conv_ref[base - back:base - back + sub, :]
        y = y * jax.nn.sigmoid(y)
        mq_ref[rows, :] = (y[:, 0:HEADS * ML_QK] * (ML_QK ** -0.5)).astype(BF16)
        mk_ref[rows, :] = y[:, HEADS * ML_QK:].astype(BF16)

        mo_ref[rows, :] = jax.nn.sigmoid(
            _dot(xn, w_ref[:, 5 * GROUP_W:6 * GROUP_W])).astype(BF16)

    conv_ref[0:halo, :] = conv_ref[tm:tm + halo, :]


def _inproj_call(x, gains, w_all, w_cols_t, rope, conv_w, conv_b, layer, batch, seq):
    rows = x.shape[0]
    tm = PROJ_ROWS
    tiles_per_seq = seq // tm
    row = lambda w: pl.BlockSpec((tm, w), lambda i: (i, 0))
    rope_spec = pl.BlockSpec((tm, LANES), lambda i: (i % tiles_per_seq, 0))
    rope_t_spec = pl.BlockSpec((DA_HEAD_DIM // 2, tm), lambda i: (0, i % tiles_per_seq))
    t_spec = pl.BlockSpec((GROUP_W, tm), lambda i: (i // tiles_per_seq, i % tiles_per_seq))
    out_shapes = (
        jax.ShapeDtypeStruct((batch * GROUP_W, seq), BF16),
        jax.ShapeDtypeStruct((rows, GROUP_W), BF16),
        jax.ShapeDtypeStruct((batch * GROUP_W, seq), BF16),
        jax.ShapeDtypeStruct((rows, HEADS * ML_QK), BF16),
        jax.ShapeDtypeStruct((rows, HEADS * ML_QK), BF16),
        jax.ShapeDtypeStruct((batch * GROUP_W, seq), BF16),
        jax.ShapeDtypeStruct((rows, GROUP_W), BF16),
        jax.ShapeDtypeStruct((batch * N_GATES, seq), F32),
    )
    gate_spec = pl.BlockSpec((N_GATES, tm), lambda i: (i // tiles_per_seq, i % tiles_per_seq))
    wt_spec = pl.BlockSpec((None, 3 * GROUP_W + BF16_ROWS, D_MODEL), lambda i: (layer, 0, 0),
                           pipeline_mode=pl.Buffered(1))
    out_specs = (t_spec, row(GROUP_W), t_spec, row(HEADS * ML_QK), row(HEADS * ML_QK),
                 t_spec, row(GROUP_W), gate_spec)
    cos, sin, cos_t, sin_t = rope
    return pl.pallas_call(
        functools.partial(_inproj_kernel, tiles_per_seq),
        grid=(rows // tm,),
        in_specs=[row(D_MODEL), _resident(gains, layer), _resident(w_all, layer),
                  wt_spec,
                  rope_spec, rope_spec, rope_t_spec, rope_t_spec,
                  _resident(conv_w, layer), _resident(conv_b, layer)],
        out_specs=out_specs,
        out_shape=out_shapes,
        scratch_shapes=[pltpu.VMEM((tm + 2 * SUBLANES, GROUP_W), F32)],
        compiler_params=_params("arbitrary"),
        name="mixer_inproj",
    )(x, gains, w_all, w_cols_t, cos, sin, cos_t, sin_t, conv_w, conv_b)


def _attn_kernel(lam_init, lam_ref, subg_ref, qt_ref, k_ref, vt_ref, o_ref,
                 m_sc, l_sc, acc_sc, s_sc, l_done, acc_done):
    blk = ATT_BLOCK
    n_blocks = qt_ref.shape[1] // blk

    lv = lam_ref[...]
    lam = (jnp.exp(jnp.sum(lv[0:1] * lv[1:2], axis=-1, keepdims=True))
           - jnp.exp(jnp.sum(lv[2:3] * lv[3:4], axis=-1, keepdims=True)) + lam_init)

    def write_out(block):
        o_t = acc_done[0] / l_done[0] - lam * (acc_done[1] / l_done[1])
        start = pl.multiple_of(block * blk, blk)
        o_ref[pl.ds(start, blk), :] = (
            _rms(o_t.T, subg_ref[...]) * (1.0 - lam_init)).astype(BF16)

    l_done[...] = jnp.ones(l_done.shape, F32)
    acc_done[...] = jnp.zeros(acc_done.shape, F32)

    def query_block(i, carry):
        q_start = pl.multiple_of(i * blk, blk)
        qt = qt_ref[:, pl.ds(q_start, blk)]
        feat = lax.broadcasted_iota(jnp.int32, qt.shape, 0)
        zero = jnp.zeros_like(qt)
        qt_halves = (jnp.where(feat < DA_HEAD_DIM, qt, zero), jnp.where(feat >= DA_HEAD_DIM, qt, zero))

        def chunk_mask():
            key = lax.broadcasted_iota(jnp.int32, (blk, blk), 0) // CHUNK
            qry = lax.broadcasted_iota(jnp.int32, (blk, blk), 1) // CHUNK
            return key <= qry

        def key_block(j):
            start = pl.multiple_of(j * blk, blk)
            return k_ref[pl.ds(start, blk), :], vt_ref[:, pl.ds(start, blk)]

        def online_step(j, mask):
            kb, vb = key_block(j)
            for c in range(2):
                s = _dot(kb, qt_halves[c])
                if mask is not None:
                    s = jnp.where(mask, s, -jnp.inf)
                m_old = m_sc[c]
                m_new = jnp.maximum(m_old, jnp.max(s, axis=0, keepdims=True))
                alpha = jnp.exp2(m_old - m_new)
                p = jnp.exp2(s - m_new)
                l_sc[c] = alpha * l_sc[c] + jnp.sum(p, axis=0, keepdims=True)
                acc_sc[c] = alpha * acc_sc[c] + _dot(vb, p.astype(BF16))
                m_sc[c] = m_new

        def scores(j, slot, mask=None):
            kb, _ = key_block(j)
            for c in range(2):
                s = _dot(kb, qt_halves[c])
                if mask is not None:
                    s = jnp.where(mask, s, -jnp.inf)
                    m_sc[c] = jnp.max(s, axis=0, keepdims=True)
                p = jnp.exp2(s - m_sc[c])
                l_sc[c] += jnp.sum(p, axis=0, keepdims=True)
                s_sc[slot, c] = p.astype(BF16)

        def accumulate(j, slot):
            _, vb = key_block(j)
            for c in range(2):
                acc_sc[c] += _dot(vb, s_sc[slot, c])

        write_out(jnp.maximum(i - 1, 0))
        l_sc[...] = jnp.zeros(l_sc.shape, F32)
        acc_sc[...] = jnp.zeros(acc_sc.shape, F32)
        scores(i, 0, chunk_mask())

        def pair(j):
            scores(j, 1)
            accumulate(jnp.where(j == 0, i, j - 1), 0)
            scores(j + 1, 0)
            accumulate(j, 1)

        def quad(t, carry):
            pair(4 * t)
            pair(4 * t + 2)
            return carry

        lax.fori_loop(0, i // 4, quad, 0)

        @pl.when(i % 4 >= 2)
        def _():
            pair(i - i % 4)

        @pl.when(i % 2 == 1)
        def _():
            scores(i - 1, 1)
            accumulate(jnp.where(i == 1, i, i - 2), 0)
            accumulate(i - 1, 1)

        @pl.when(i % 2 == 0)
        def _():
            accumulate(jnp.where(i == 0, i, i - 1), 0)

        in_range = jnp.all(l_sc[...] < ATT_L_LIMIT)

        @pl.when(jnp.logical_not(in_range))
        def _():
            m_sc[...] = jnp.full(m_sc.shape, -jnp.inf, F32)
            l_sc[...] = jnp.zeros(l_sc.shape, F32)
            acc_sc[...] = jnp.zeros(acc_sc.shape, F32)
            online_step(i, chunk_mask())

            def body(j, carry):
                online_step(j, None)
                return carry

            lax.fori_loop(0, i, body, 0)

        l_done[...] = l_sc[...]
        acc_done[...] = acc_sc[...]
        return carry

    lax.fori_loop(0, n_blocks, query_block, 0)
    write_out(n_blocks - 1)


def _attn_call(qt, k, vt, lam_vecs, subln_g, layer, lam_init, batch, seq):
    tq = ATT_BLOCK
    t_spec = pl.BlockSpec((LANES, seq), lambda b, h: (b * HEADS + h, 0))
    row_spec = pl.BlockSpec((seq, LANES), lambda b, h: (b, h))
    return pl.pallas_call(
        functools.partial(_attn_kernel, lam_init),
        grid=(batch, HEADS),
        in_specs=[_resident(lam_vecs, layer), _resident(subln_g, layer), t_spec, row_spec, t_spec],
        out_specs=row_spec,
        out_shape=jax.ShapeDtypeStruct(k.shape, BF16),
        scratch_shapes=[pltpu.VMEM((2, 1, tq), F32), pltpu.VMEM((2, 1, tq), F32),
                        pltpu.VMEM((2, LANES, tq), F32), pltpu.VMEM((2, 2, tq, tq), BF16),
                        pltpu.VMEM((2, 1, tq), F32), pltpu.VMEM((2, LANES, tq), F32)],
        compiler_params=_params("arbitrary", "arbitrary"),
        name="diff_attention",
    )(lam_vecs, subln_g, qt, k, vt)


def _split3(x):
    hi = x.astype(BF16)
    r = x - hi.astype(F32)
    mid = r.astype(BF16)
    lo = (r - mid.astype(F32)).astype(BF16)
    return hi, mid, lo


def _chunk_gates(raw, bias, tri):
    z = raw + bias
    log_f = jnp.minimum(z, 0.0) - jnp.log1p(jnp.exp(-jnp.abs(z)))
    hi, mid, lo = _split3(log_f)
    b = (_dot(hi, tri) + _dot(mid, tri) + _dot(lo, tri))[HEADS:]
    return z[0:HEADS] - b, b


def _mlstm_kernel(q_ref, k_ref, vt_ref, og_ref, gates_ref, bias_ref, normg_ref, o_ref,
                  state_ref, m_ref):
    blk = ML_BLOCK

    @pl.when(pl.program_id(1) == 0)
    def _():
        state_ref[...] = jnp.zeros(state_ref.shape, F32)
        m_ref[...] = jnp.zeros(m_ref.shape, F32)

    src = lax.broadcasted_iota(jnp.int32, (blk, blk), 0)
    dst = lax.broadcasted_iota(jnp.int32, (blk, blk), 1)
    causal = src <= dst
    lane = lax.broadcasted_iota(jnp.int32, (blk, LANES), 1)
    ones_rows = jnp.ones((HEAD_V, blk), BF16)
    tri = jnp.where(causal, 1.0, 0.0).astype(BF16)
    normg = normg_ref[...]
    bias = bias_ref[...]

    for chunk in range(ML_STEP_CHUNKS):
        rows = slice(chunk * blk, (chunk + 1) * blk)
        g_all, b_all = _chunk_gates(gates_ref[:, rows], bias, tri)
        for h in range(HEADS):
            pair = (h // 2) * LANES
            in_head = (lane // ML_QK) == (h % 2)
            q2 = q_ref[rows, pair:pair + LANES]
            k2 = k_ref[rows, pair:pair + LANES]
            kh = jnp.where(in_head, k2, jnp.zeros_like(k2))
            v_ext = jnp.concatenate([vt_ref[h * HEAD_V:(h + 1) * HEAD_V, rows], ones_rows], axis=0)

            b_row = b_all[h:h + 1]
            g_lanes = jnp.broadcast_to(g_all[h:h + 1], (LANES, blk)).T
            g_src = jnp.concatenate([g_lanes] * (blk // LANES), axis=1)
            m_prev = m_ref[h:h + 1, 0:1]

            run_max = jnp.max(jnp.where(causal, g_src, -jnp.inf), axis=0, keepdims=True)
            m_run = jnp.maximum(m_prev, run_max)
            s_t = _dot_nt(kh, q2)
            w_t = (jnp.where(causal, jnp.exp(g_src - m_run), 0.0) * s_t).astype(BF16)
            state = state_ref[h]
            nd = _dot(v_ext, w_t) + jnp.exp(m_prev - m_run) * _dot_nt(state.astype(BF16), q2)
            num = nd[0:HEAD_V]
            den = nd[HEAD_V:]
            hid = num / jnp.maximum(jnp.abs(den), jnp.exp(-(b_row + m_run)))
            hid = hid * lax.rsqrt(jnp.mean(hid * hid, axis=0, keepdims=True) + RMS_EPS)
            out = hid.T * normg * og_ref[rows, h * HEAD_V:(h + 1) * HEAD_V].astype(F32)
            o_ref[rows, h * HEAD_V:(h + 1) * HEAD_V] = out.astype(BF16)

            m_last = m_run[:, blk - 1:blk]
            kw = jnp.where(in_head, k2.astype(F32) * jnp.exp(g_lanes - m_last), 0.0)
            state_ref[h] = jnp.exp(m_prev - m_last) * state + _dot(v_ext, kw.astype(BF16))
            m_ref[h:h + 1, 0:1] = b_row[:, blk - 1:blk] + m_last


def _mlstm_call(q, k, vt, og, gates_t, gate_b, norm_g, layer, batch, seq):
    step = ML_STEP_CHUNKS * ML_BLOCK
    ns = seq // step
    row = lambda w: pl.BlockSpec((step, w), lambda b, c: (b * ns + c, 0))
    return pl.pallas_call(
        _mlstm_kernel,
        grid=(batch, ns),
        in_specs=[row(HEADS * ML_QK), row(HEADS * ML_QK),
                  pl.BlockSpec((GROUP_W, step), lambda b, c: (b, c)), row(GROUP_W),
                  pl.BlockSpec((N_GATES, step), lambda b, c: (b, c)),
                  _resident(gate_b, layer), _resident(norm_g, layer)],
        out_specs=row(GROUP_W),
        out_shape=jax.ShapeDtypeStruct(og.shape, BF16),
        scratch_shapes=[pltpu.VMEM((HEADS, 2 * HEAD_V, LANES), F32),
                        pltpu.VMEM((SUBLANES, LANES), F32)],
        compiler_params=_params("arbitrary", "arbitrary"),
        name="mlstm",
    )(q, k, vt, og, gates_t, gate_b, norm_g)


def _rope_tables(seq):
    inv_freq = ROPE_THETA ** (-jnp.arange(0, DA_HEAD_DIM, 2, dtype=F32) / DA_HEAD_DIM)
    ang = jnp.arange(seq, dtype=F32)[:, None] * inv_freq[None, :]
    cos, sin = jnp.cos(ang), jnp.sin(ang)
    reps = LANES // DA_HEAD_DIM
    cos_l = jnp.tile(jnp.concatenate([cos, cos], axis=1), (1, reps))
    sin_l = jnp.tile(jnp.concatenate([-sin, sin], axis=1), (1, reps))
    return cos_l, sin_l, cos.T, sin.T


def _mixer(x, p, layer, lam_init, rope, batch, seq):
    qt, k, vt, mq, mk, mv, mo, gates_t = _inproj_call(
        x, p["gains"], p["w_all"], p["w_cols_t"], rope, p["conv_w"], p["conv_b"],
        layer, batch, seq)
    y_da = _attn_call(qt, k, vt, p["lam"], p["subln_g"], layer, lam_init, batch, seq)
    y_ml = _mlstm_call(mq, mk, mv, mo, gates_t, p["gate_b"], p["norm_g"], layer, batch, seq)
    return y_da, y_ml


def _prepare(norm_gains, mix_w_in, mix_w_out, da_lambda, da_subln_g, ml_conv_w, ml_conv_b,
             ml_gate_b, ml_norm_g):
    gate_pad = jnp.zeros((DEPTH, D_MODEL, LANES - N_GATES), F32)
    return {
        "gains": norm_gains,
        "w_all": mix_w_in.astype(BF16),
        "w_cols_t": jnp.concatenate([mix_w_in[:, :, 0:GROUP_W],
                                     mix_w_in[:, :, 2 * GROUP_W:3 * GROUP_W],
                                     mix_w_in[:, :, 4 * GROUP_W:5 * GROUP_W],
                                     mix_w_in[:, :, N_MAIN:], gate_pad],
                                    axis=2).transpose(0, 2, 1).astype(BF16),
        "w_mix_out": mix_w_out,
        "lam": da_lambda,
        "subln_g": da_subln_g[:, None, :],
        "conv_w": ml_conv_w,
        "conv_b": ml_conv_b[:, None, :],
        "gate_b": ml_gate_b[:, :, None],
        "norm_g": ml_norm_g[:, None, :],
    }


def kernel(x, ffn_w_in, ffn_w_out, norm_gains, mix_w_in, mix_w_out, da_lambda, da_subln_g,
           ml_conv_w, ml_conv_b, ml_gate_b, ml_norm_g):
    batch, seq, d = x.shape
    assert d == D_MODEL and (batch * seq) % FFN_ROWS == 0 and seq % PROJ_ROWS == 0
    assert seq % (ML_BLOCK * ML_STEP_CHUNKS) == 0
    assert seq % ATT_BLOCK == 0
    rope = _rope_tables(seq)
    p = _prepare(norm_gains, mix_w_in, mix_w_out, da_lambda, da_subln_g, ml_conv_w, ml_conv_b,
                 ml_gate_b, ml_norm_g)
    h = x.reshape(batch * seq, d)
    for l in range(DEPTH):
        h = _ffn_call(h, norm_gains, ffn_w_in, ffn_w_out, l)
        lam_init = 0.8 - 0.6 * math.exp(-0.3 * l)
        y_da, y_ml = _mixer(h, p, l, lam_init, rope, batch, seq)
        h = _ffn_call(h, norm_gains, ffn_w_in, ffn_w_out, l, mix=(y_da, y_ml, p["w_mix_out"]))
    return h.reshape(batch, seq, d)
```

```python
import functools
import math

import jax
import jax.numpy as jnp
from jax import lax
from jax.experimental import pallas as pl
from jax.experimental.pallas import tpu as pltpu

D_MODEL = 1024
DEPTH = 2
CHUNK = 64
ROPE_THETA = 10000.0
RMS_EPS = 1e-6
D_FF = 2816
HEADS = 4
DA_HEAD_DIM = 64
HEAD_V = 128
GROUP_W = HEADS * HEAD_V
ML_QK = 64
ML_CONV = 4
N_MAIN = 6 * GROUP_W
N_GATES = 2 * HEADS
Q_SCALE = DA_HEAD_DIM ** -0.5 * math.log2(math.e)

LANES = 128
SUBLANES = 8
BF16_ROWS = 16
VMEM_LIMIT = 58 * 1024 * 1024

FFN_ROWS = 512
PROJ_ROWS = 1024
FFN_COLS = 256
ROW_SPLIT = 4
ATT_BLOCK = 512
ATT_L_LIMIT = 2.0 ** 64
ML_BLOCK = 256
ML_STEP_CHUNKS = 4

BF16 = jnp.bfloat16
F32 = jnp.float32


def _dot(a, b):
    return jnp.dot(a, b, preferred_element_type=F32)


def _dot_nt(a, b):
    return lax.dot_general(a, b, (((1,), (1,)), ((), ())), preferred_element_type=F32)


def _rms(x, g):
    return x * lax.rsqrt(jnp.mean(x * x, axis=-1, keepdims=True) + RMS_EPS) * g


def _params(*sem):
    return pltpu.CompilerParams(dimension_semantics=sem, vmem_limit_bytes=VMEM_LIMIT)


def _resident(arr, *lead):
    tail = arr.shape[len(lead):]
    index = tuple(lead) + (0,) * len(tail)
    return pl.BlockSpec((None,) * len(lead) + tail, lambda *_: index,
                        pipeline_mode=pl.Buffered(1))


def _ffn_body(x, g_pre, g_post, win_ref, wout_ref):
    xn = _rms(x, g_pre).astype(BF16)
    acc = jnp.zeros(x.shape, F32)
    for c in range(D_FF // FFN_COLS):
        lo = c * FFN_COLS
        gate = _dot(xn, win_ref[:, lo:lo + FFN_COLS].astype(BF16))
        up = _dot(xn, win_ref[:, D_FF + lo:D_FF + lo + FFN_COLS].astype(BF16))
        act = (gate * jax.nn.sigmoid(gate) * up).astype(BF16)
        acc = acc + _dot(act, wout_ref[lo:lo + FFN_COLS, :].astype(BF16))
    return x + 0.5 * _rms(acc, g_post)


def _ffn_kernel(x_ref, gains_ref, win_ref, wout_ref, o_ref):
    o_ref[...] = _ffn_body(x_ref[...], gains_ref[0:1], gains_ref[1:2], win_ref, wout_ref)


def _mix_ffn_kernel(x_ref, yda_ref, yml_ref, wmix_ref, gains_ref, win_ref, wout_ref, o_ref):
    h = (_dot(yda_ref[...], wmix_ref[0:GROUP_W, :].astype(BF16))
         + _dot(yml_ref[...], wmix_ref[GROUP_W:, :].astype(BF16)))
    x = x_ref[...] + _rms(h, gains_ref[3:4])
    o_ref[...] = _ffn_body(x, gains_ref[4:5], gains_ref[5:6], win_ref, wout_ref)


def _ffn_call(x, gains, w_in, w_out, layer, mix=None):
    rows = x.shape[0]
    tm = FFN_ROWS
    which = 0 if mix is None else 1
    row_spec = pl.BlockSpec((tm, D_MODEL), lambda i: (i, 0))
    half_spec = pl.BlockSpec((tm, GROUP_W), lambda i: (i, 0))
    weights = [_resident(gains, layer), _resident(w_in, layer, which),
               _resident(w_out, layer, which)]
    if mix is None:
        kern, ins, specs = _ffn_kernel, (x, gains, w_in, w_out), [row_spec] + weights
    else:
        y_da, y_ml, w_mix = mix
        kern = _mix_ffn_kernel
        ins = (x, y_da, y_ml, w_mix, gains, w_in, w_out)
        specs = [row_spec, half_spec, half_spec, _resident(w_mix, layer)] + weights
    return pl.pallas_call(
        kern,
        grid=(rows // tm,),
        in_specs=specs,
        out_specs=row_spec,
        out_shape=jax.ShapeDtypeStruct(x.shape, F32),
        compiler_params=_params("arbitrary"),
        name="ffn" if mix is None else "mix_ffn",
    )(*ins)


def _rope(t, cos, sin_signed, first_half):
    swapped = jnp.where(first_half, pltpu.roll(t, LANES - 32, axis=1), pltpu.roll(t, 32, axis=1))
    return t * cos + swapped * sin_signed


def _rope_rows(t, cos, sin):
    half = DA_HEAD_DIM // 2
    a, b = t[0:half], t[half:]
    return jnp.concatenate([a * cos - b * sin, b * cos + a * sin], axis=0)


def _inproj_kernel(tiles_per_seq, x_ref, gain_ref, w_ref, wt_ref, cos_ref, sin_ref,
                   cost_ref, sint_ref, convw_ref, convb_ref,
                   qt_ref, k_ref, vt_ref, mq_ref, mk_ref, mv_ref, mo_ref, gates_ref,
                   conv_ref):
    tm = x_ref.shape[0]
    halo = SUBLANES
    sub = tm // ROW_SPLIT

    @pl.when(pl.program_id(0) % tiles_per_seq == 0)
    def _():
        conv_ref[0:halo, :] = jnp.zeros((halo, GROUP_W), F32)

    lane = lax.broadcasted_iota(jnp.int32, (sub, LANES), 1)
    first_half = (lane % 64) < 32

    for part in range(ROW_SPLIT):
        r0 = part * sub
        rows = slice(r0, r0 + sub)
        xn = _rms(x_ref[rows, :], gain_ref[2:3]).astype(BF16)

        qv_t = _dot_nt(wt_ref[...], xn)
        gates_ref[:, rows] = qv_t[3 * GROUP_W:3 * GROUP_W + N_GATES, :]
        cos_t = cost_ref[:, rows]
        sin_t = sint_ref[:, rows]
        for r in range(GROUP_W // DA_HEAD_DIM):
            lo = r * DA_HEAD_DIM
            qt_ref[lo:lo + DA_HEAD_DIM, rows] = (
                _rope_rows(qv_t[lo:lo + DA_HEAD_DIM], cos_t, sin_t) * Q_SCALE).astype(BF16)
        vt_ref[:, rows] = qv_t[GROUP_W:2 * GROUP_W, :].astype(BF16)
        mv_ref[:, rows] = qv_t[2 * GROUP_W:3 * GROUP_W, :].astype(BF16)

        cos = cos_ref[rows, :]
        sin = sin_ref[rows, :]
        kk = _dot(xn, w_ref[:, GROUP_W:2 * GROUP_W])
        for h in range(HEADS):
            lo = h * LANES
            k_ref[rows, lo:lo + LANES] = _rope(kk[:, lo:lo + LANES], cos, sin, first_half).astype(BF16)

        base = halo + r0
        conv_ref[base:base + sub, :] = _dot(xn, w_ref[:, 3 * GROUP_W:4 * GROUP_W])
        y = convb_ref[...] + convw_ref[ML_CONV - 1:ML_CONV, :] * conv_ref[base:base + sub, :]
        for j in range(ML_CONV - 1):
            back = ML_CONV - 1 - j
            y = y + convw_ref[j:j + 1, :] * conv_ref[base - back:base - back + sub, :]
        y = y * jax.nn.sigmoid(y)
        mq_ref[rows, :] = (y[:, 0:HEADS * ML_QK] * (ML_QK ** -0.5)).astype(BF16)
        mk_ref[rows, :] = y[:, HEADS * ML_QK:].astype(BF16)

        mo_ref[rows, :] = jax.nn.sigmoid(
            _dot(xn, w_ref[:, 5 * GROUP_W:6 * GROUP_W])).astype(BF16)

    conv_ref[0:halo, :] = conv_ref[tm:tm + halo, :]


def _inproj_call(x, gains, w_all, w_cols_t, rope, conv_w, conv_b, layer, batch, seq):
    rows = x.shape[0]
    tm = PROJ_ROWS
    tiles_per_seq = seq // tm
    row = lambda w: pl.BlockSpec((tm, w), lambda i: (i, 0))
    rope_spec = pl.BlockSpec((tm, LANES), lambda i: (i % tiles_per_seq, 0))
    rope_t_spec = pl.BlockSpec((DA_HEAD_DIM // 2, tm), lambda i: (0, i % tiles_per_seq))
    t_spec = pl.BlockSpec((GROUP_W, tm), lambda i: (i // tiles_per_seq, i % tiles_per_seq))
    out_shapes = (
        jax.ShapeDtypeStruct((batch * GROUP_W, seq), BF16),
        jax.ShapeDtypeStruct((rows, GROUP_W), BF16),
        jax.ShapeDtypeStruct((batch * GROUP_W, seq), BF16),
        jax.ShapeDtypeStruct((rows, HEADS * ML_QK), BF16),
        jax.ShapeDtypeStruct((rows, HEADS * ML_QK), BF16),
        jax.ShapeDtypeStruct((batch * GROUP_W, seq), BF16),
        jax.ShapeDtypeStruct((rows, GROUP_W), BF16),
        jax.ShapeDtypeStruct((batch * N_GATES, seq), F32),
    )
    gate_spec = pl.BlockSpec((N_GATES, tm), lambda i: (i // tiles_per_seq, i % tiles_per_seq))
    wt_spec = pl.BlockSpec((None, 3 * GROUP_W + BF16_ROWS, D_MODEL), lambda i: (layer, 0, 0),
                           pipeline_mode=pl.Buffered(1))
    out_specs = (t_spec, row(GROUP_W), t_spec, row(HEADS * ML_QK), row(HEADS * ML_QK),
                 t_spec, row(GROUP_W), gate_spec)
    cos, sin, cos_t, sin_t = rope
    return pl.pallas_call(
        functools.partial(_inproj_kernel, tiles_per_seq),
        grid=(rows // tm,),
        in_specs=[row(D_MODEL), _resident(gains, layer), _resident(w_all, layer),
                  wt_spec,
                  rope_spec, rope_spec, rope_t_spec, rope_t_spec,
                  _resident(conv_w, layer), _resident(conv_b, layer)],
        out_specs=out_specs,
        out_shape=out_shapes,
        scratch_shapes=[pltpu.VMEM((tm + 2 * SUBLANES, GROUP_W), F32)],
        compiler_params=_params("arbitrary"),
        name="mixer_inproj",
    )(x, gains, w_all, w_cols_t, cos, sin, cos_t, sin_t, conv_w, conv_b)


def _attn_kernel(lam_init, lam_ref, subg_ref, qt_ref, k_ref, vt_ref, o_ref,
                 m_sc, l_sc, acc_sc, s_sc, l_done, acc_done):
    blk = ATT_BLOCK
    n_blocks = qt_ref.shape[1] // blk

    lv = lam_ref[...]
    lam = (jnp.exp(jnp.sum(lv[0:1] * lv[1:2], axis=-1, keepdims=True))
           - jnp.exp(jnp.sum(lv[2:3] * lv[3:4], axis=-1, keepdims=True)) + lam_init)

    def write_out(block):
        o_t = acc_done[0] / l_done[0] - lam * (acc_done[1] / l_done[1])
        start = pl.multiple_of(block * blk, blk)
        o_ref[pl.ds(start, blk), :] = (
            _rms(o_t.T, subg_ref[...]) * (1.0 - lam_init)).astype(BF16)

    l_done[...] = jnp.ones(l_done.shape, F32)
    acc_done[...] = jnp.zeros(acc_done.shape, F32)

    def query_block(i, carry):
        q_start = pl.multiple_of(i * blk, blk)
        qt = qt_ref[:, pl.ds(q_start, blk)]
        feat = lax.broadcasted_iota(jnp.int32, qt.shape, 0)
        zero = jnp.zeros_like(qt)
        qt_halves = (jnp.where(feat < DA_HEAD_DIM, qt, zero), jnp.where(feat >= DA_HEAD_DIM, qt, zero))

        def chunk_mask():
            key = lax.broadcasted_iota(jnp.int32, (blk, blk), 0) // CHUNK
            qry = lax.broadcasted_iota(jnp.int32, (blk, blk), 1) // CHUNK
            return key <= qry

        def key_block(j):
            start = pl.multiple_of(j * blk, blk)
            return k_ref[pl.ds(start, blk), :], vt_ref[:, pl.ds(start, blk)]

        def online_step(j, mask):
            kb, vb = key_block(j)
            for c in range(2):
                s = _dot(kb, qt_halves[c])
                if mask is not None:
                    s = jnp.where(mask, s, -jnp.inf)
                m_old = m_sc[c]
                m_new = jnp.maximum(m_old, jnp.max(s, axis=0, keepdims=True))
                alpha = jnp.exp2(m_old - m_new)
                p = jnp.exp2(s - m_new)
                l_sc[c] = alpha * l_sc[c] + jnp.sum(p, axis=0, keepdims=True)
                acc_sc[c] = alpha * acc_sc[c] + _dot(vb, p.astype(BF16))
                m_sc[c] = m_new

        def scores(j, slot, mask=None):
            kb, _ = key_block(j)
            for c in range(2):
                s = _dot(kb, qt_halves[c])
                if mask is not None:
                    s = jnp.where(mask, s, -jnp.inf)
                    m_sc[c] = jnp.max(s, axis=0, keepdims=True)
                p = jnp.exp2(s - m_sc[c])
                l_sc[c] += jnp.sum(p, axis=0, keepdims=True)
                s_sc[slot, c] = p.astype(BF16)

        def accumulate(j, slot):
            _, vb = key_block(j)
            for c in range(2):
                acc_sc[c] += _dot(vb, s_sc[slot, c])

        write_out(jnp.maximum(i - 1, 0))
        l_sc[...] = jnp.zeros(l_sc.shape, F32)
        acc_sc[...] = jnp.zeros(acc_sc.shape, F32)
        scores(i, 0, chunk_mask())

        def pair(j):
            scores(j, 1)
            accumulate(jnp.where(j == 0, i, j - 1), 0)
            scores(j + 1, 0)
            accumulate(j, 1)

        def octet(t, carry):
            for u in range(4):
                pair(8 * t + 2 * u)
            return carry

        lax.fori_loop(0, i // 8, octet, 0)
        left = i % 8

        @pl.when(left >= 4)
        def _():
            pair(i - left)
            pair(i - left + 2)

        @pl.when(left % 4 >= 2)
        def _():
            pair(i - left % 4)

        @pl.when(i % 2 == 1)
        def _():
            scores(i - 1, 1)
            accumulate(jnp.where(i == 1, i, i - 2), 0)
            accumulate(i - 1, 1)

        @pl.when(i % 2 == 0)
        def _():
            accumulate(jnp.where(i == 0, i, i - 1), 0)

        in_range = jnp.all(l_sc[...] < ATT_L_LIMIT)

        @pl.when(jnp.logical_not(in_range))
        def _():
            m_sc[...] = jnp.full(m_sc.shape, -jnp.inf, F32)
            l_sc[...] = jnp.zeros(l_sc.shape, F32)
            acc_sc[...] = jnp.zeros(acc_sc.shape, F32)
            online_step(i, chunk_mask())

            def body(j, carry):
                online_step(j, None)
                return carry

            lax.fori_loop(0, i, body, 0)

        l_done[...] = l_sc[...]
        acc_done[...] = acc_sc[...]
        return carry

    lax.fori_loop(0, n_blocks, query_block, 0)
    write_out(n_blocks - 1)


def _attn_call(qt, k, vt, lam_vecs, subln_g, layer, lam_init, batch, seq):
    tq = ATT_BLOCK
    t_spec = pl.BlockSpec((LANES, seq), lambda b, h: (b * HEADS + h, 0))
    row_spec = pl.BlockSpec((seq, LANES), lambda b, h: (b, h))
    return pl.pallas_call(
        functools.partial(_attn_kernel, lam_init),
        grid=(batch, HEADS),
        in_specs=[_resident(lam_vecs, layer), _resident(subln_g, layer), t_spec, row_spec, t_spec],
        out_specs=row_spec,
        out_shape=jax.ShapeDtypeStruct(k.shape, BF16),
        scratch_shapes=[pltpu.VMEM((2, 1, tq), F32), pltpu.VMEM((2, 1, tq), F32),
                        pltpu.VMEM((2, LANES, tq), F32), pltpu.VMEM((2, 2, tq, tq), BF16),
                        pltpu.VMEM((2, 1, tq), F32), pltpu.VMEM((2, LANES, tq), F32)],
        compiler_params=_params("arbitrary", "arbitrary"),
        name="diff_attention",
    )(lam_vecs, subln_g, qt, k, vt)


def _split3(x):
    hi = x.astype(BF16)
    r = x - hi.astype(F32)
    mid = r.astype(BF16)
    lo = (r - mid.astype(F32)).astype(BF16)
    return hi, mid, lo


def _chunk_gates(raw, bias, tri):
    z = raw + bias
    log_f = jnp.minimum(z, 0.0) - jnp.log1p(jnp.exp(-jnp.abs(z)))
    hi, mid, lo = _split3(log_f)
    b = (_dot(hi, tri) + _dot(mid, tri) + _dot(lo, tri))[HEADS:]
    return z[0:HEADS] - b, b


def _mlstm_kernel(q_ref, k_ref, vt_ref, og_ref, gates_ref, bias_ref, normg_ref, o_ref,
                  state_ref, m_ref):
    blk = ML_BLOCK

    @pl.when(pl.program_id(1) == 0)
    def _():
        state_ref[...] = jnp.zeros(state_ref.shape, F32)
        m_ref[...] = jnp.zeros(m_ref.shape, F32)

    src = lax.broadcasted_iota(jnp.int32, (blk, blk), 0)
    dst = lax.broadcasted_iota(jnp.int32, (blk, blk), 1)
    causal = src <= dst
    lane = lax.broadcasted_iota(jnp.int32, (blk, LANES), 1)
    ones_rows = jnp.ones((HEAD_V, blk), BF16)
    tri = jnp.where(causal, 1.0, 0.0).astype(BF16)
    normg = normg_ref[...]
    bias = bias_ref[...]

    for chunk in range(ML_STEP_CHUNKS):
        rows = slice(chunk * blk, (chunk + 1) * blk)
        g_all, b_all = _chunk_gates(gates_ref[:, rows], bias, tri)
        for h in range(HEADS):
            pair = (h // 2) * LANES
            in_head = (lane // ML_QK) == (h % 2)
            q2 = q_ref[rows, pair:pair + LANES]
            k2 = k_ref[rows, pair:pair + LANES]
            kh = jnp.where(in_head, k2, jnp.zeros_like(k2))
            v_ext = jnp.concatenate([vt_ref[h * HEAD_V:(h + 1) * HEAD_V, rows], ones_rows], axis=0)

            b_row = b_all[h:h + 1]
            g_lanes = jnp.broadcast_to(g_all[h:h + 1], (LANES, blk)).T
            g_src = jnp.concatenate([g_lanes] * (blk // LANES), axis=1)
            m_prev = m_ref[h:h + 1, 0:1]

            run_max = jnp.max(jnp.where(causal, g_src, -jnp.inf), axis=0, keepdims=True)
            m_run = jnp.maximum(m_prev, run_max)
            s_t = _dot_nt(kh, q2)
            w_t = (jnp.where(causal, jnp.exp(g_src - m_run), 0.0) * s_t).astype(BF16)
            state = state_ref[h]
            nd = _dot(v_ext, w_t) + jnp.exp(m_prev - m_run) * _dot_nt(state.astype(BF16), q2)
            num = nd[0:HEAD_V]
            den = nd[HEAD_V:]
            hid = num / jnp.maximum(jnp.abs(den), jnp.exp(-(b_row + m_run)))
            hid = hid * lax.rsqrt(jnp.mean(hid * hid, axis=0, keepdims=True) + RMS_EPS)
            out = hid.T * normg * og_ref[rows, h * HEAD_V:(h + 1) * HEAD_V].astype(F32)
            o_ref[rows, h * HEAD_V:(h + 1) * HEAD_V] = out.astype(BF16)

            m_last = m_run[:, blk - 1:blk]
            kw = jnp.where(in_head, k2.astype(F32) * jnp.exp(g_lanes - m_last), 0.0)
            state_ref[h] = jnp.exp(m_prev - m_last) * state + _dot(v_ext, kw.astype(BF16))
            m_ref[h:h + 1, 0:1] = b_row[:, blk - 1:blk] + m_last


def _mlstm_call(q, k, vt, og, gates_t, gate_b, norm_g, layer, batch, seq):
    step = ML_STEP_CHUNKS * ML_BLOCK
    ns = seq // step
    row = lambda w: pl.BlockSpec((step, w), lambda b, c: (b * ns + c, 0))
    return pl.pallas_call(
        _mlstm_kernel,
        grid=(batch, ns),
        in_specs=[row(HEADS * ML_QK), row(HEADS * ML_QK),
                  pl.BlockSpec((GROUP_W, step), lambda b, c: (b, c)), row(GROUP_W),
                  pl.BlockSpec((N_GATES, step), lambda b, c: (b, c)),
                  _resident(gate_b, layer), _resident(norm_g, layer)],
        out_specs=row(GROUP_W),
        out_shape=jax.ShapeDtypeStruct(og.shape, BF16),
        scratch_shapes=[pltpu.VMEM((HEADS, 2 * HEAD_V, LANES), F32),
                        pltpu.VMEM((SUBLANES, LANES), F32)],
        compiler_params=_params("arbitrary", "arbitrary"),
        name="mlstm",
    )(q, k, vt, og, gates_t, gate_b, norm_g)


def _rope_tables(seq):
    inv_freq = ROPE_THETA ** (-jnp.arange(0, DA_HEAD_DIM, 2, dtype=F32) / DA_HEAD_DIM)
    ang = jnp.arange(seq, dtype=F32)[:, None] * inv_freq[None, :]
    cos, sin = jnp.cos(ang), jnp.sin(ang)
    reps = LANES // DA_HEAD_DIM
    cos_l = jnp.tile(jnp.concatenate([cos, cos], axis=1), (1, reps))
    sin_l = jnp.tile(jnp.concatenate([-sin, sin], axis=1), (1, reps))
    return cos_l, sin_l, cos.T, sin.T


def _mixer(x, p, layer, lam_init, rope, batch, seq):
    qt, k, vt, mq, mk, mv, mo, gates_t = _inproj_call(
        x, p["gains"], p["w_all"], p["w_cols_t"], rope, p["conv_w"], p["conv_b"],
        layer, batch, seq)
    y_da = _attn_call(qt, k, vt, p["lam"], p["subln_g"], layer, lam_init, batch, seq)
    y_ml = _mlstm_call(mq, mk, mv, mo, gates_t, p["gate_b"], p["norm_g"], layer, batch, seq)
    return y_da, y_ml


def _prepare(norm_gains, mix_w_in, mix_w_out, da_lambda, da_subln_g, ml_conv_w, ml_conv_b,
             ml_gate_b, ml_norm_g):
    gate_pad = jnp.zeros((DEPTH, D_MODEL, LANES - N_GATES), F32)
    return {
        "gains": norm_gains,
        "w_all": mix_w_in.astype(BF16),
        "w_cols_t": jnp.concatenate([mix_w_in[:, :, 0:GROUP_W],
                                     mix_w_in[:, :, 2 * GROUP_W:3 * GROUP_W],
                                     mix_w_in[:, :, 4 * GROUP_W:5 * GROUP_W],
                                     mix_w_in[:, :, N_MAIN:], gate_pad],
                                    axis=2).transpose(0, 2, 1).astype(BF16),
        "w_mix_out": mix_w_out,
        "lam": da_lambda,
        "subln_g": da_subln_g[:, None, :],
        "conv_w": ml_conv_w,
        "conv_b": ml_conv_b[:, None, :],
        "gate_b": ml_gate_b[:, :, None],
        "norm_g": ml_norm_g[:, None, :],
    }


def kernel(x, ffn_w_in, ffn_w_out, norm_gains, mix_w_in, mix_w_out, da_lambda, da_subln_g,
           ml_conv_w, ml_conv_b, ml_gate_b, ml_norm_g):
    batch, seq, d = x.shape
    assert d == D_MODEL and (batch * seq) % FFN_ROWS == 0 and seq % PROJ_ROWS == 0
    assert seq % (ML_BLOCK * ML_STEP_CHUNKS) == 0
    assert seq % ATT_BLOCK == 0
    rope = _rope_tables(seq)
    p = _prepare(norm_gains, mix_w_in, mix_w_out, da_lambda, da_subln_g, ml_conv_w, ml_conv_b,
                 ml_gate_b, ml_norm_g)
    h = x.reshape(batch * seq, d)
    for l in range(DEPTH):
        h = _ffn_call(h, norm_gains, ffn_w_in, ffn_w_out, l)
        lam_init = 0.8 - 0.6 * math.exp(-0.3 * l)
        y_da, y_ml = _mixer(h, p, l, lam_init, rope, batch, seq)
        h = _ffn_call(h, norm_gains, ffn_w_in, ffn_w_out, l, mix=(y_da, y_ml, p["w_mix_out"]))
    return h.reshape(batch, seq, d)
```

```python
import functools
import math

import jax
import jax.numpy as jnp
from jax import lax
from jax.experimental import pallas as pl
from jax.experimental.pallas import tpu as pltpu

D_MODEL = 1024
DEPTH = 2
CHUNK = 64
ROPE_THETA = 10000.0
RMS_EPS = 1e-6
D_FF = 2816
HEADS = 4
DA_HEAD_DIM = 64
HEAD_V = 128
GROUP_W = HEADS * HEAD_V
ML_QK = 64
ML_CONV = 4
N_MAIN = 6 * GROUP_W
N_GATES = 2 * HEADS
Q_SCALE = DA_HEAD_DIM ** -0.5 * math.log2(math.e)

LANES = 128
SUBLANES = 8
BF16_ROWS = 16
VMEM_LIMIT = 58 * 1024 * 1024

FFN_ROWS = 512
PROJ_ROWS = 1024
FFN_COLS = 256
ROW_SPLIT = 4
ATT_BLOCK = 512
ATT_L_LIMIT = 2.0 ** 64
ML_BLOCK = 256
ML_STEP_CHUNKS = 8

BF16 = jnp.bfloat16
F32 = jnp.float32


def _dot(a, b):
    return jnp.dot(a, b, preferred_element_type=F32)


def _dot_nt(a, b):
    return lax.dot_general(a, b, (((1,), (1,)), ((), ())), preferred_element_type=F32)


def _rms(x, g):
    return x * lax.rsqrt(jnp.mean(x * x, axis=-1, keepdims=True) + RMS_EPS) * g


def _params(*sem):
    return pltpu.CompilerParams(dimension_semantics=sem, vmem_limit_bytes=VMEM_LIMIT)


def _resident(arr, *lead):
    tail = arr.shape[len(lead):]
    index = tuple(lead) + (0,) * len(tail)
    return pl.BlockSpec((None,) * len(lead) + tail, lambda *_: index,
                        pipeline_mode=pl.Buffered(1))


def _ffn_body(x, g_pre, g_post, win_ref, wout_ref):
    xn = _rms(x, g_pre).astype(BF16)
    acc = jnp.zeros(x.shape, F32)
    for c in range(D_FF // FFN_COLS):
        lo = c * FFN_COLS
        gate = _dot(xn, win_ref[:, lo:lo + FFN_COLS].astype(BF16))
        up = _dot(xn, win_ref[:, D_FF + lo:D_FF + lo + FFN_COLS].astype(BF16))
        act = (gate * jax.nn.sigmoid(gate) * up).astype(BF16)
        acc = acc + _dot(act, wout_ref[lo:lo + FFN_COLS, :].astype(BF16))
    return x + 0.5 * _rms(acc, g_post)


def _ffn_kernel(x_ref, gains_ref, win_ref, wout_ref, o_ref):
    o_ref[...] = _ffn_body(x_ref[...], gains_ref[0:1], gains_ref[1:2], win_ref, wout_ref)


def _mix_ffn_kernel(x_ref, yda_ref, yml_ref, wmix_ref, gains_ref, win_ref, wout_ref, o_ref):
    h = (_dot(yda_ref[...], wmix_ref[0:GROUP_W, :].astype(BF16))
         + _dot(yml_ref[...], wmix_ref[GROUP_W:, :].astype(BF16)))
    x = x_ref[...] + _rms(h, gains_ref[3:4])
    o_ref[...] = _ffn_body(x, gains_ref[4:5], gains_ref[5:6], win_ref, wout_ref)


def _ffn_call(x, gains, w_in, w_out, layer, mix=None):
    rows = x.shape[0]
    tm = FFN_ROWS
    which = 0 if mix is None else 1
    row_spec = pl.BlockSpec((tm, D_MODEL), lambda i: (i, 0))
    half_spec = pl.BlockSpec((tm, GROUP_W), lambda i: (i, 0))
    weights = [_resident(gains, layer), _resident(w_in, layer, which),
               _resident(w_out, layer, which)]
    if mix is None:
        kern, ins, specs = _ffn_kernel, (x, gains, w_in, w_out), [row_spec] + weights
    else:
        y_da, y_ml, w_mix = mix
        kern = _mix_ffn_kernel
        ins = (x, y_da, y_ml, w_mix, gains, w_in, w_out)
        specs = [row_spec, half_spec, half_spec, _resident(w_mix, layer)] + weights
    return pl.pallas_call(
        kern,
        grid=(rows // tm,),
        in_specs=specs,
        out_specs=row_spec,
        out_shape=jax.ShapeDtypeStruct(x.shape, F32),
        compiler_params=_params("arbitrary"),
        name="ffn" if mix is None else "mix_ffn",
    )(*ins)


def _rope(t, cos, sin_signed, first_half):
    half = DA_HEAD_DIM // 2
    swapped = jnp.where(first_half, pltpu.roll(t, LANES - half, axis=1), pltpu.roll(t, half, axis=1))
    return t * cos + swapped * sin_signed


def _rope_rows(t, cos, sin):
    half = DA_HEAD_DIM // 2
    a, b = t[0:half], t[half:]
    return jnp.concatenate([a * cos - b * sin, b * cos + a * sin], axis=0)


def _inproj_kernel(tiles_per_seq, x_ref, gain_ref, w_ref, wt_ref, cos_ref, sin_ref,
                   cost_ref, sint_ref, convw_ref, convb_ref,
                   qt_ref, k_ref, vt_ref, mq_ref, mk_ref, mv_ref, mo_ref, gates_ref,
                   conv_ref):
    tm = x_ref.shape[0]
    halo = SUBLANES
    sub = tm // ROW_SPLIT

    @pl.when(pl.program_id(0) % tiles_per_seq == 0)
    def _():
        conv_ref[0:halo, :] = jnp.zeros((halo, GROUP_W), F32)

    lane = lax.broadcasted_iota(jnp.int32, (sub, LANES), 1)
    first_half = (lane % DA_HEAD_DIM) < DA_HEAD_DIM // 2

    for part in range(ROW_SPLIT):
        r0 = part * sub
        rows = slice(r0, r0 + sub)
        xn = _rms(x_ref[rows, :], gain_ref[2:3]).astype(BF16)

        qv_t = _dot_nt(wt_ref[...], xn)
        gates_ref[:, rows] = qv_t[3 * GROUP_W:3 * GROUP_W + N_GATES, :]
        cos_t = cost_ref[:, rows]
        sin_t = sint_ref[:, rows]
        for r in range(GROUP_W // DA_HEAD_DIM):
            lo = r * DA_HEAD_DIM
            qt_ref[lo:lo + DA_HEAD_DIM, rows] = (
                _rope_rows(qv_t[lo:lo + DA_HEAD_DIM], cos_t, sin_t) * Q_SCALE).astype(BF16)
        vt_ref[:, rows] = qv_t[GROUP_W:2 * GROUP_W, :].astype(BF16)
        mv_ref[:, rows] = qv_t[2 * GROUP_W:3 * GROUP_W, :].astype(BF16)

        cos = cos_ref[rows, :]
        sin = sin_ref[rows, :]
        kk = _dot(xn, w_ref[:, GROUP_W:2 * GROUP_W])
        for h in range(HEADS):
            lo = h * LANES
            k_ref[rows, lo:lo + LANES] = _rope(kk[:, lo:lo + LANES], cos, sin, first_half).astype(BF16)

        base = halo + r0
        conv_ref[base:base + sub, :] = _dot(xn, w_ref[:, 3 * GROUP_W:4 * GROUP_W])
        y = convb_ref[...] + convw_ref[ML_CONV - 1:ML_CONV, :] * conv_ref[base:base + sub, :]
        for j in range(ML_CONV - 1):
            back = ML_CONV - 1 - j
            y = y + convw_ref[j:j + 1, :] * conv_ref[base - back:base - back + sub, :]
        y = y * jax.nn.sigmoid(y)
        mq_ref[rows, :] = (y[:, 0:HEADS * ML_QK] * (ML_QK ** -0.5)).astype(BF16)
        mk_ref[rows, :] = y[:, HEADS * ML_QK:].astype(BF16)

        mo_ref[rows, :] = jax.nn.sigmoid(
            _dot(xn, w_ref[:, 5 * GROUP_W:6 * GROUP_W])).astype(BF16)

    conv_ref[0:halo, :] = conv_ref[tm:tm + halo, :]


def _inproj_call(x, gains, w_all, w_cols_t, rope, conv_w, conv_b, layer, batch, seq):
    rows = x.shape[0]
    tm = PROJ_ROWS
    tiles_per_seq = seq // tm
    row = lambda w: pl.BlockSpec((tm, w), lambda i: (i, 0))
    rope_spec = pl.BlockSpec((tm, LANES), lambda i: (i % tiles_per_seq, 0))
    rope_t_spec = pl.BlockSpec((DA_HEAD_DIM // 2, tm), lambda i: (0, i % tiles_per_seq))
    t_spec = pl.BlockSpec((GROUP_W, tm), lambda i: (i // tiles_per_seq, i % tiles_per_seq))
    out_shapes = (
        jax.ShapeDtypeStruct((batch * GROUP_W, seq), BF16),
        jax.ShapeDtypeStruct((rows, GROUP_W), BF16),
        jax.ShapeDtypeStruct((batch * GROUP_W, seq), BF16),
        jax.ShapeDtypeStruct((rows, HEADS * ML_QK), BF16),
        jax.ShapeDtypeStruct((rows, HEADS * ML_QK), BF16),
        jax.ShapeDtypeStruct((batch * GROUP_W, seq), BF16),
        jax.ShapeDtypeStruct((rows, GROUP_W), BF16),
        jax.ShapeDtypeStruct((batch * N_GATES, seq), F32),
    )
    gate_spec = pl.BlockSpec((N_GATES, tm), lambda i: (i // tiles_per_seq, i % tiles_per_seq))
    wt_spec = pl.BlockSpec((None, 3 * GROUP_W + BF16_ROWS, D_MODEL), lambda i: (layer, 0, 0),
                           pipeline_mode=pl.Buffered(1))
    out_specs = (t_spec, row(GROUP_W), t_spec, row(HEADS * ML_QK), row(HEADS * ML_QK),
                 t_spec, row(GROUP_W), gate_spec)
    cos, sin, cos_t, sin_t = rope
    return pl.pallas_call(
        functools.partial(_inproj_kernel, tiles_per_seq),
        grid=(rows // tm,),
        in_specs=[row(D_MODEL), _resident(gains, layer), _resident(w_all, layer),
                  wt_spec,
                  rope_spec, rope_spec, rope_t_spec, rope_t_spec,
                  _resident(conv_w, layer), _resident(conv_b, layer)],
        out_specs=out_specs,
        out_shape=out_shapes,
        scratch_shapes=[pltpu.VMEM((tm + 2 * SUBLANES, GROUP_W), F32)],
        compiler_params=_params("arbitrary"),
        name="mixer_inproj",
    )(x, gains, w_all, w_cols_t, cos, sin, cos_t, sin_t, conv_w, conv_b)


def _attn_kernel(lam_init, lam_ref, subg_ref, qt_ref, k_ref, vt_ref, o_ref,
                 m_sc, l_sc, acc_sc, s_sc, l_done, acc_done):
    blk = ATT_BLOCK
    n_blocks = qt_ref.shape[1] // blk

    lv = lam_ref[...]
    lam = (jnp.exp(jnp.sum(lv[0:1] * lv[1:2], axis=-1, keepdims=True))
           - jnp.exp(jnp.sum(lv[2:3] * lv[3:4], axis=-1, keepdims=True)) + lam_init)

    def write_out(block):
        o_t = acc_done[0] / l_done[0] - lam * (acc_done[1] / l_done[1])
        start = pl.multiple_of(block * blk, blk)
        o_ref[pl.ds(start, blk), :] = (
            _rms(o_t.T, subg_ref[...]) * (1.0 - lam_init)).astype(BF16)

    l_done[...] = jnp.ones(l_done.shape, F32)
    acc_done[...] = jnp.zeros(acc_done.shape, F32)

    def query_block(i, carry):
        q_start = pl.multiple_of(i * blk, blk)
        qt = qt_ref[:, pl.ds(q_start, blk)]
        feat = lax.broadcasted_iota(jnp.int32, qt.shape, 0)
        zero = jnp.zeros_like(qt)
        qt_halves = (jnp.where(feat < DA_HEAD_DIM, qt, zero), jnp.where(feat >= DA_HEAD_DIM, qt, zero))

        def chunk_mask():
            key = lax.broadcasted_iota(jnp.int32, (blk, blk), 0) // CHUNK
            qry = lax.broadcasted_iota(jnp.int32, (blk, blk), 1) // CHUNK
            return key <= qry

        def key_block(j):
            start = pl.multiple_of(j * blk, blk)
            return k_ref[pl.ds(start, blk), :], vt_ref[:, pl.ds(start, blk)]

        def online_step(j, mask):
            kb, vb = key_block(j)
            for c in range(2):
                s = _dot(kb, qt_halves[c])
                if mask is not None:
                    s = jnp.where(mask, s, -jnp.inf)
                m_old = m_sc[c]
                m_new = jnp.maximum(m_old, jnp.max(s, axis=0, keepdims=True))
                alpha = jnp.exp2(m_old - m_new)
                p = jnp.exp2(s - m_new)
                l_sc[c] = alpha * l_sc[c] + jnp.sum(p, axis=0, keepdims=True)
                acc_sc[c] = alpha * acc_sc[c] + _dot(vb, p.astype(BF16))
                m_sc[c] = m_new

        def scores(j, slot, mask=None):
            kb, _ = key_block(j)
            for c in range(2):
                s = _dot(kb, qt_halves[c])
                if mask is not None:
                    s = jnp.where(mask, s, -jnp.inf)
                    m_sc[c] = jnp.max(s, axis=0, keepdims=True)
                p = jnp.exp2(s - m_sc[c])
                l_sc[c] += jnp.sum(p, axis=0, keepdims=True)
                s_sc[slot, c] = p.astype(BF16)

        def accumulate(j, slot):
            _, vb = key_block(j)
            for c in range(2):
                acc_sc[c] += _dot(vb, s_sc[slot, c])

        write_out(jnp.maximum(i - 1, 0))
        l_sc[...] = jnp.zeros(l_sc.shape, F32)
        acc_sc[...] = jnp.zeros(acc_sc.shape, F32)
        scores(i, 0, chunk_mask())

        def pair(j):
            scores(j, 1)
            accumulate(jnp.where(j == 0, i, j - 1), 0)
            scores(j + 1, 0)
            accumulate(j, 1)

        def octet(t, carry):
            for u in range(4):
                pair(8 * t + 2 * u)
            return carry

        lax.fori_loop(0, i // 8, octet, 0)
        left = i % 8

        @pl.when(left >= 4)
        def _():
            pair(i - left)
            pair(i - left + 2)

        @pl.when(left % 4 >= 2)
        def _():
            pair(i - left % 4)

        @pl.when(i % 2 == 1)
        def _():
            scores(i - 1, 1)
            accumulate(jnp.where(i == 1, i, i - 2), 0)
            accumulate(i - 1, 1)

        @pl.when(i % 2 == 0)
        def _():
            accumulate(jnp.where(i == 0, i, i - 1), 0)

        in_range = jnp.all(l_sc[...] < ATT_L_LIMIT)

        @pl.when(jnp.logical_not(in_range))
        def _():
            m_sc[...] = jnp.full(m_sc.shape, -jnp.inf, F32)
            l_sc[...] = jnp.zeros(l_sc.shape, F32)
            acc_sc[...] = jnp.zeros(acc_sc.shape, F32)
            online_step(i, chunk_mask())

            def body(j, carry):
                online_step(j, None)
                return carry

            lax.fori_loop(0, i, body, 0)

        l_done[...] = l_sc[...]
        acc_done[...] = acc_sc[...]
        return carry

    lax.fori_loop(0, n_blocks, query_block, 0)
    write_out(n_blocks - 1)


def _attn_call(qt, k, vt, lam_vecs, subln_g, layer, lam_init, batch, seq):
    tq = ATT_BLOCK
    t_spec = pl.BlockSpec((LANES, seq), lambda b, h: (b * HEADS + h, 0))
    row_spec = pl.BlockSpec((seq, LANES), lambda b, h: (b, h))
    return pl.pallas_call(
        functools.partial(_attn_kernel, lam_init),
        grid=(batch, HEADS),
        in_specs=[_resident(lam_vecs, layer), _resident(subln_g, layer), t_spec, row_spec, t_spec],
        out_specs=row_spec,
        out_shape=jax.ShapeDtypeStruct(k.shape, BF16),
        scratch_shapes=[pltpu.VMEM((2, 1, tq), F32), pltpu.VMEM((2, 1, tq), F32),
                        pltpu.VMEM((2, LANES, tq), F32), pltpu.VMEM((2, 2, tq, tq), BF16),
                        pltpu.VMEM((2, 1, tq), F32), pltpu.VMEM((2, LANES, tq), F32)],
        compiler_params=_params("arbitrary", "arbitrary"),
        name="diff_attention",
    )(lam_vecs, subln_g, qt, k, vt)


def _split3(x):
    hi = x.astype(BF16)
    r = x - hi.astype(F32)
    mid = r.astype(BF16)
    lo = (r - mid.astype(F32)).astype(BF16)
    return hi, mid, lo


def _chunk_gates(raw, bias, tri):
    z = raw + bias
    log_f = jnp.minimum(z, 0.0) - jnp.log1p(jnp.exp(-jnp.abs(z)))
    hi, mid, lo = _split3(log_f)
    b = (_dot(hi, tri) + _dot(mid, tri) + _dot(lo, tri))[HEADS:]
    return z[0:HEADS] - b, b


def _mlstm_kernel(q_ref, k_ref, vt_ref, og_ref, gates_ref, bias_ref, normg_ref, o_ref,
                  state_ref, m_ref):
    blk = ML_BLOCK

    @pl.when(pl.program_id(1) == 0)
    def _():
        state_ref[...] = jnp.zeros(state_ref.shape, F32)
        m_ref[...] = jnp.zeros(m_ref.shape, F32)

    src = lax.broadcasted_iota(jnp.int32, (blk, blk), 0)
    dst = lax.broadcasted_iota(jnp.int32, (blk, blk), 1)
    causal = src <= dst
    lane = lax.broadcasted_iota(jnp.int32, (blk, LANES), 1)
    ones_rows = jnp.ones((HEAD_V, blk), BF16)
    tri = jnp.where(causal, 1.0, 0.0).astype(BF16)
    normg = normg_ref[...]
    bias = bias_ref[...]

    for chunk in range(ML_STEP_CHUNKS):
        rows = slice(chunk * blk, (chunk + 1) * blk)
        g_all, b_all = _chunk_gates(gates_ref[:, rows], bias, tri)
        for h in range(HEADS):
            pair = (h // 2) * LANES
            in_head = (lane // ML_QK) == (h % 2)
            q2 = q_ref[rows, pair:pair + LANES]
            k2 = k_ref[rows, pair:pair + LANES]
            kh = jnp.where(in_head, k2, jnp.zeros_like(k2))
            v_ext = jnp.concatenate([vt_ref[h * HEAD_V:(h + 1) * HEAD_V, rows], ones_rows], axis=0)

            b_row = b_all[h:h + 1]
            g_lanes = jnp.broadcast_to(g_all[h:h + 1], (LANES, blk)).T
            g_src = jnp.concatenate([g_lanes] * (blk // LANES), axis=1)
            m_prev = m_ref[h:h + 1, 0:1]

            run_max = jnp.max(jnp.where(causal, g_src, -jnp.inf), axis=0, keepdims=True)
            m_run = jnp.maximum(m_prev, run_max)
            s_t = _dot_nt(kh, q2)
            w_t = (jnp.where(causal, jnp.exp(g_src - m_run), 0.0) * s_t).astype(BF16)
            state = state_ref[h]
            nd = _dot(v_ext, w_t) + jnp.exp(m_prev - m_run) * _dot_nt(state.astype(BF16), q2)
            num = nd[0:HEAD_V]
            den = nd[HEAD_V:]
            hid = num / jnp.maximum(jnp.abs(den), jnp.exp(-(b_row + m_run)))
            hid = hid * lax.rsqrt(jnp.mean(hid * hid, axis=0, keepdims=True) + RMS_EPS)
            out = hid.T * normg * og_ref[rows, h * HEAD_V:(h + 1) * HEAD_V].astype(F32)
            o_ref[rows, h * HEAD_V:(h + 1) * HEAD_V] = out.astype(BF16)

            m_last = m_run[:, blk - 1:blk]
            kw = jnp.where(in_head, k2.astype(F32) * jnp.exp(g_lanes - m_last), 0.0)
            state_ref[h] = jnp.exp(m_prev - m_last) * state + _dot(v_ext, kw.astype(BF16))
            m_ref[h:h + 1, 0:1] = b_row[:, blk - 1:blk] + m_last


def _mlstm_call(q, k, vt, og, gates_t, gate_b, norm_g, layer, batch, seq):
    step = ML_STEP_CHUNKS * ML_BLOCK
    ns = seq // step
    row = lambda w: pl.BlockSpec((step, w), lambda b, c: (b * ns + c, 0))
    return pl.pallas_call(
        _mlstm_kernel,
        grid=(batch, ns),
        in_specs=[row(HEADS * ML_QK), row(HEADS * ML_QK),
                  pl.BlockSpec((GROUP_W, step), lambda b, c: (b, c)), row(GROUP_W),
                  pl.BlockSpec((N_GATES, step), lambda b, c: (b, c)),
                  _resident(gate_b, layer), _resident(norm_g, layer)],
        out_specs=row(GROUP_W),
        out_shape=jax.ShapeDtypeStruct(og.shape, BF16),
        scratch_shapes=[pltpu.VMEM((HEADS, 2 * HEAD_V, LANES), F32),
                        pltpu.VMEM((SUBLANES, LANES), F32)],
        compiler_params=_params("arbitrary", "arbitrary"),
        name="mlstm",
    )(q, k, vt, og, gates_t, gate_b, norm_g)


def _rope_tables(seq):
    inv_freq = ROPE_THETA ** (-jnp.arange(0, DA_HEAD_DIM, 2, dtype=F32) / DA_HEAD_DIM)
    ang = jnp.arange(seq, dtype=F32)[:, None] * inv_freq[None, :]
    cos, sin = jnp.cos(ang), jnp.sin(ang)
    reps = LANES // DA_HEAD_DIM
    cos_l = jnp.tile(jnp.concatenate([cos, cos], axis=1), (1, reps))
    sin_l = jnp.tile(jnp.concatenate([-sin, sin], axis=1), (1, reps))
    return cos_l, sin_l, cos.T, sin.T


def _mixer(x, p, layer, lam_init, rope, batch, seq):
    qt, k, vt, mq, mk, mv, mo, gates_t = _inproj_call(
        x, p["gains"], p["w_all"], p["w_cols_t"], rope, p["conv_w"], p["conv_b"],
        layer, batch, seq)
    y_da = _attn_call(qt, k, vt, p["lam"], p["subln_g"], layer, lam_init, batch, seq)
    y_ml = _mlstm_call(mq, mk, mv, mo, gates_t, p["gate_b"], p["norm_g"], layer, batch, seq)
    return y_da, y_ml


def _prepare(norm_gains, mix_w_in, mix_w_out, da_lambda, da_subln_g, ml_conv_w, ml_conv_b,
             ml_gate_b, ml_norm_g):
    gate_pad = jnp.zeros((DEPTH, D_MODEL, LANES - N_GATES), F32)
    return {
        "gains": norm_gains,
        "w_all": mix_w_in.astype(BF16),
        "w_cols_t": jnp.concatenate([mix_w_in[:, :, 0:GROUP_W],
                                     mix_w_in[:, :, 2 * GROUP_W:3 * GROUP_W],
                                     mix_w_in[:, :, 4 * GROUP_W:5 * GROUP_W],
                                     mix_w_in[:, :, N_MAIN:], gate_pad],
                                    axis=2).transpose(0, 2, 1).astype(BF16),
        "w_mix_out": mix_w_out,
        "lam": da_lambda,
        "subln_g": da_subln_g[:, None, :],
        "conv_w": ml_conv_w,
        "conv_b": ml_conv_b[:, None, :],
        "gate_b": ml_gate_b[:, :, None],
        "norm_g": ml_norm_g[:, None, :],
    }


def kernel(x, ffn_w_in, ffn_w_out, norm_gains, mix_w_in, mix_w_out, da_lambda, da_subln_g,
           ml_conv_w, ml_conv_b, ml_gate_b, ml_norm_g):
    batch, seq, d = x.shape
    assert d == D_MODEL and (batch * seq) % FFN_ROWS == 0 and seq % PROJ_ROWS == 0
    assert seq % (ML_BLOCK * ML_STEP_CHUNKS) == 0
    assert seq % ATT_BLOCK == 0
    rope = _rope_tables(seq)
    p = _prepare(norm_gains, mix_w_in, mix_w_out, da_lambda, da_subln_g, ml_conv_w, ml_conv_b,
                 ml_gate_b, ml_norm_g)
    h = x.reshape(batch * seq, d)
    for l in range(DEPTH):
        h = _ffn_call(h, norm_gains, ffn_w_in, ffn_w_out, l)
        lam_init = 0.8 - 0.6 * math.exp(-0.3 * l)
        y_da, y_ml = _mixer(h, p, l, lam_init, rope, batch, seq)
        h = _ffn_call(h, norm_gains, ffn_w_in, ffn_w_out, l, mix=(y_da, y_ml, p["w_mix_out"]))
    return h.reshape(batch, seq, d)
```

```python
import functools
import math

import jax
import jax.numpy as jnp
from jax import lax
from jax.experimental import pallas as pl
from jax.experimental.pallas import tpu as pltpu

D_MODEL = 1024
DEPTH = 2
CHUNK = 64
ROPE_THETA = 10000.0
RMS_EPS = 1e-6
D_FF = 2816
HEADS = 4
DA_HEAD_DIM = 64
HEAD_V = 128
GROUP_W = HEADS * HEAD_V
ML_QK = 64
ML_CONV = 4
N_MAIN = 6 * GROUP_W
N_GATES = 2 * HEADS
Q_SCALE = DA_HEAD_DIM ** -0.5 * math.log2(math.e)

LANES = 128
SUBLANES = 8
BF16_ROWS = 16
VMEM_LIMIT = 58 * 1024 * 1024

FFN_ROWS = 512
PROJ_ROWS = 1024
FFN_COLS = 256
ROW_SPLIT = 4
ATT_BLOCK = 512
ATT_L_LIMIT = 2.0 ** 64
ML_BLOCK = 256
ML_STEP_CHUNKS = 4

BF16 = jnp.bfloat16
F32 = jnp.float32


def _dot(a, b):
    return jnp.dot(a, b, preferred_element_type=F32)


def _dot_nt(a, b):
    return lax.dot_general(a, b, (((1,), (1,)), ((), ())), preferred_element_type=F32)


def _rms(x, g):
    return x * lax.rsqrt(jnp.mean(x * x, axis=-1, keepdims=True) + RMS_EPS) * g


def _params(*sem):
    return pltpu.CompilerParams(dimension_semantics=sem, vmem_limit_bytes=VMEM_LIMIT)


def _resident(arr, *lead):
    tail = arr.shape[len(lead):]
    index = tuple(lead) + (0,) * len(tail)
    return pl.BlockSpec((None,) * len(lead) + tail, lambda *_: index,
                        pipeline_mode=pl.Buffered(1))


def _ffn_body(x, g_pre, g_post, win_ref, wout_ref):
    xn = _rms(x, g_pre).astype(BF16)
    acc = jnp.zeros(x.shape, F32)
    for c in range(D_FF // FFN_COLS):
        lo = c * FFN_COLS
        gate = _dot(xn, win_ref[:, lo:lo + FFN_COLS].astype(BF16))
        up = _dot(xn, win_ref[:, D_FF + lo:D_FF + lo + FFN_COLS].astype(BF16))
        act = (gate * jax.nn.sigmoid(gate) * up).astype(BF16)
        acc = acc + _dot(act, wout_ref[lo:lo + FFN_COLS, :].astype(BF16))
    return x + 0.5 * _rms(acc, g_post)


def _ffn_kernel(x_ref, gains_ref, win_ref, wout_ref, o_ref):
    o_ref[...] = _ffn_body(x_ref[...], gains_ref[0:1], gains_ref[1:2], win_ref, wout_ref)


def _mix_ffn_kernel(x_ref, yda_ref, yml_ref, wmix_ref, gains_ref, win_ref, wout_ref, o_ref):
    h = (_dot(yda_ref[...], wmix_ref[0:GROUP_W, :].astype(BF16))
         + _dot(yml_ref[...], wmix_ref[GROUP_W:, :].astype(BF16)))
    x = x_ref[...] + _rms(h, gains_ref[3:4])
    o_ref[...] = _ffn_body(x, gains_ref[4:5], gains_ref[5:6], win_ref, wout_ref)


def _ffn_call(x, gains, w_in, w_out, layer, mix=None):
    rows = x.shape[0]
    tm = FFN_ROWS
    which = 0 if mix is None else 1
    row_spec = pl.BlockSpec((tm, D_MODEL), lambda i: (i, 0))
    half_spec = pl.BlockSpec((tm, GROUP_W), lambda i: (i, 0))
    weights = [_resident(gains, layer), _resident(w_in, layer, which),
               _resident(w_out, layer, which)]
    if mix is None:
        kern, ins, specs = _ffn_kernel, (x, gains, w_in, w_out), [row_spec] + weights
    else:
        y_da, y_ml, w_mix = mix
        kern = _mix_ffn_kernel
        ins = (x, y_da, y_ml, w_mix, gains, w_in, w_out)
        specs = [row_spec, half_spec, half_spec, _resident(w_mix, layer)] + weights
    return pl.pallas_call(
        kern,
        grid=(rows // tm,),
        in_specs=specs,
        out_specs=row_spec,
        out_shape=jax.ShapeDtypeStruct(x.shape, F32),
        compiler_params=_params("arbitrary"),
        name="ffn" if mix is None else "mix_ffn",
    )(*ins)


def _rope(t, cos, sin_signed, first_half):
    swapped = jnp.where(first_half, pltpu.roll(t, LANES - 32, axis=1), pltpu.roll(t, 32, axis=1))
    return t * cos + swapped * sin_signed


def _rope_rows(t, cos, sin):
    half = DA_HEAD_DIM // 2
    a, b = t[0:half], t[half:]
    return jnp.concatenate([a * cos - b * sin, b * cos + a * sin], axis=0)


def _inproj_kernel(tiles_per_seq, x_ref, gain_ref, w_ref, wt_ref, cos_ref, sin_ref,
                   cost_ref, sint_ref, convw_ref, convb_ref,
                   qt_ref, k_ref, vt_ref, mq_ref, mk_ref, mv_ref, mo_ref, gates_ref,
                   conv_ref):
    tm = x_ref.shape[0]
    halo = SUBLANES
    sub = tm // ROW_SPLIT

    @pl.when(pl.program_id(0) % tiles_per_seq == 0)
    def _():
        conv_ref[0:halo, :] = jnp.zeros((halo, GROUP_W), F32)

    lane = lax.broadcasted_iota(jnp.int32, (sub, LANES), 1)
    first_half = (lane % 64) < 32

    for part in range(ROW_SPLIT):
        r0 = part * sub
        rows = slice(r0, r0 + sub)
        xn = _rms(x_ref[rows, :], gain_ref[2:3]).astype(BF16)

        qv_t = _dot_nt(wt_ref[...], xn)
        gates_ref[:, rows] = qv_t[3 * GROUP_W:3 * GROUP_W + N_GATES, :]
        cos_t = cost_ref[:, rows]
        sin_t = sint_ref[:, rows]
        for r in range(GROUP_W // DA_HEAD_DIM):
            lo = r * DA_HEAD_DIM
            qt_ref[lo:lo + DA_HEAD_DIM, rows] = (
                _rope_rows(qv_t[lo:lo + DA_HEAD_DIM], cos_t, sin_t) * Q_SCALE).astype(BF16)
        vt_ref[:, rows] = qv_t[GROUP_W:2 * GROUP_W, :].astype(BF16)
        mv_ref[:, rows] = qv_t[2 * GROUP_W:3 * GROUP_W, :].astype(BF16)

        cos = cos_ref[rows, :]
        sin = sin_ref[rows, :]
        kk = _dot(xn, w_ref[:, GROUP_W:2 * GROUP_W])
        for h in range(HEADS):
            lo = h * LANES
            k_ref[rows, lo:lo + LANES] = _rope(kk[:, lo:lo + LANES], cos, sin, first_half).astype(BF16)

        base = halo + r0
        conv_ref[base:base + sub, :] = _dot(xn, w_ref[:, 3 * GROUP_W:4 * GROUP_W])
        y = convb_ref[...] + convw_ref[ML_CONV - 1:ML_CONV, :] * conv_ref[base:base + sub, :]
        for j in range(ML_CONV - 1):
            back = ML_CONV - 1 - j
            y = y + convw_ref[j:j + 1, :] * conv_ref[base - back:base - back + sub, :]
        y = y * jax.nn.sigmoid(y)
        mq_ref[rows, :] = (y[:, 0:HEADS * ML_QK] * (ML_QK ** -0.5)).astype(BF16)
        mk_ref[rows, :] = y[:, HEADS * ML_QK:].astype(BF16)

        mo_ref[rows, :] = jax.nn.sigmoid(
            _dot(xn, w_ref[:, 5 * GROUP_W:6 * GROUP_W])).astype(BF16)

    conv_ref[0:halo, :] = conv_ref[tm:tm + halo, :]


def _inproj_call(x, gains, w_all, w_cols_t, rope, conv_w, conv_b, layer, batch, seq):
    rows = x.shape[0]
    tm = PROJ_ROWS
    tiles_per_seq = seq // tm
    row = lambda w: pl.BlockSpec((tm, w), lambda i: (i, 0))
    rope_spec = pl.BlockSpec((tm, LANES), lambda i: (i % tiles_per_seq, 0))
    rope_t_spec = pl.BlockSpec((DA_HEAD_DIM // 2, tm), lambda i: (0, i % tiles_per_seq))
    t_spec = pl.BlockSpec((GROUP_W, tm), lambda i: (i // tiles_per_seq, i % tiles_per_seq))
    out_shapes = (
        jax.ShapeDtypeStruct((batch * GROUP_W, seq), BF16),
        jax.ShapeDtypeStruct((rows, GROUP_W), BF16),
        jax.ShapeDtypeStruct((batch * GROUP_W, seq), BF16),
        jax.ShapeDtypeStruct((rows, HEADS * ML_QK), BF16),
        jax.ShapeDtypeStruct((rows, HEADS * ML_QK), BF16),
        jax.ShapeDtypeStruct((batch * GROUP_W, seq), BF16),
        jax.ShapeDtypeStruct((rows, GROUP_W), BF16),
        jax.ShapeDtypeStruct((batch * N_GATES, seq), F32),
    )
    gate_spec = pl.BlockSpec((N_GATES, tm), lambda i: (i // tiles_per_seq, i % tiles_per_seq))
    wt_spec = pl.BlockSpec((None, 3 * GROUP_W + BF16_ROWS, D_MODEL), lambda i: (layer, 0, 0),
                           pipeline_mode=pl.Buffered(1))
    out_specs = (t_spec, row(GROUP_W), t_spec, row(HEADS * ML_QK), row(HEADS * ML_QK),
                 t_spec, row(GROUP_W), gate_spec)
    cos, sin, cos_t, sin_t = rope
    return pl.pallas_call(
        functools.partial(_inproj_kernel, tiles_per_seq),
        grid=(rows // tm,),
        in_specs=[row(D_MODEL), _resident(gains, layer), _resident(w_all, layer),
                  wt_spec,
                  rope_spec, rope_spec, rope_t_spec, rope_t_spec,
                  _resident(conv_w, layer), _resident(conv_b, layer)],
        out_specs=out_specs,
        out_shape=out_shapes,
        scratch_shapes=[pltpu.VMEM((tm + 2 * SUBLANES, GROUP_W), F32)],
        compiler_params=_params("arbitrary"),
        name="mixer_inproj",
    )(x, gains, w_all, w_cols_t, cos, sin, cos_t, sin_t, conv_w, conv_b)


def _attn_kernel(lam_init, lam_ref, subg_ref, qt_ref, k_ref, vt_ref, o_ref,
                 m_sc, l_sc, acc_sc, s_sc):
    blk = ATT_BLOCK
    n_blocks = qt_ref.shape[1] // blk

    lv = lam_ref[...]
    lam = (jnp.exp(jnp.sum(lv[0:1] * lv[1:2], axis=-1, keepdims=True))
           - jnp.exp(jnp.sum(lv[2:3] * lv[3:4], axis=-1, keepdims=True)) + lam_init)

    def write_out(block):
        o_t = acc_sc[0] / l_sc[0] - lam * (acc_sc[1] / l_sc[1])
        start = pl.multiple_of(block * blk, blk)
        o_ref[pl.ds(start, blk), :] = (
            _rms(o_t.T, subg_ref[...]) * (1.0 - lam_init)).astype(BF16)

    l_sc[...] = jnp.ones(l_sc.shape, F32)
    acc_sc[...] = jnp.zeros(acc_sc.shape, F32)

    def query_block(i, carry):
        q_start = pl.multiple_of(i * blk, blk)
        qt = qt_ref[:, pl.ds(q_start, blk)]
        feat = lax.broadcasted_iota(jnp.int32, qt.shape, 0)
        zero = jnp.zeros_like(qt)
        qt_halves = (jnp.where(feat < DA_HEAD_DIM, qt, zero), jnp.where(feat >= DA_HEAD_DIM, qt, zero))

        def chunk_mask():
            key = lax.broadcasted_iota(jnp.int32, (blk, blk), 0) // CHUNK
            qry = lax.broadcasted_iota(jnp.int32, (blk, blk), 1) // CHUNK
            return key <= qry

        def key_block(j):
            start = pl.multiple_of(j * blk, blk)
            return k_ref[pl.ds(start, blk), :], vt_ref[:, pl.ds(start, blk)]

        def online_step(j, mask):
            kb, vb = key_block(j)
            for c in range(2):
                s = _dot(kb, qt_halves[c])
                if mask is not None:
                    s = jnp.where(mask, s, -jnp.inf)
                m_old = m_sc[c]
                m_new = jnp.maximum(m_old, jnp.max(s, axis=0, keepdims=True))
                alpha = jnp.exp2(m_old - m_new)
                p = jnp.exp2(s - m_new)
                l_sc[c] = alpha * l_sc[c] + jnp.sum(p, axis=0, keepdims=True)
                acc_sc[c] = alpha * acc_sc[c] + _dot(vb, p.astype(BF16))
                m_sc[c] = m_new

        def scores(j, slot, mask=None):
            kb, _ = key_block(j)
            for c in range(2):
                s = _dot(kb, qt_halves[c])
                if mask is not None:
                    s = jnp.where(mask, s, -jnp.inf)
                    m_sc[c] = jnp.max(s, axis=0, keepdims=True)
                p = jnp.exp2(s - m_sc[c])
                l_sc[c] += jnp.sum(p, axis=0, keepdims=True)
                s_sc[slot, c] = p.astype(BF16)

        def accumulate(j, slot):
            _, vb = key_block(j)
            for c in range(2):
                acc_sc[c] += _dot(vb, s_sc[slot, c])

        write_out(jnp.maximum(i - 1, 0))
        l_sc[...] = jnp.zeros(l_sc.shape, F32)
        acc_sc[...] = jnp.zeros(acc_sc.shape, F32)
        scores(i, 0, chunk_mask())

        def pair(j):
            scores(j, 1)
            accumulate(jnp.where(j == 0, i, j - 1), 0)
            scores(j + 1, 0)
            accumulate(j, 1)

        def octet(t, carry):
            for u in range(4):
                pair(8 * t + 2 * u)
            return carry

        lax.fori_loop(0, i // 8, octet, 0)
        left = i % 8

        @pl.when(left >= 4)
        def _():
            pair(i - left)
            pair(i - left + 2)

        @pl.when(left % 4 >= 2)
        def _():
            pair(i - left % 4)

        @pl.when(i % 2 == 1)
        def _():
            scores(i - 1, 1)
            accumulate(jnp.where(i == 1, i, i - 2), 0)
            accumulate(i - 1, 1)

        @pl.when(i % 2 == 0)
        def _():
            accumulate(jnp.where(i == 0, i, i - 1), 0)

        in_range = jnp.all(l_sc[...] < ATT_L_LIMIT)

        @pl.when(jnp.logical_not(in_range))
        def _():
            m_sc[...] = jnp.full(m_sc.shape, -jnp.inf, F32)
            l_sc[...] = jnp.zeros(l_sc.shape, F32)
            acc_sc[...] = jnp.zeros(acc_sc.shape, F32)
            online_step(i, chunk_mask())

            def body(j, carry):
                online_step(j, None)
                return carry

            lax.fori_loop(0, i, body, 0)

        return carry

    lax.fori_loop(0, n_blocks, query_block, 0)
    write_out(n_blocks - 1)


def _attn_call(qt, k, vt, lam_vecs, subln_g, layer, lam_init, batch, seq):
    tq = ATT_BLOCK
    t_spec = pl.BlockSpec((LANES, seq), lambda b, h: (b * HEADS + h, 0))
    row_spec = pl.BlockSpec((seq, LANES), lambda b, h: (b, h))
    return pl.pallas_call(
        functools.partial(_attn_kernel, lam_init),
        grid=(batch, HEADS),
        in_specs=[_resident(lam_vecs, layer), _resident(subln_g, layer), t_spec, row_spec, t_spec],
        out_specs=row_spec,
        out_shape=jax.ShapeDtypeStruct(k.shape, BF16),
        scratch_shapes=[pltpu.VMEM((2, 1, tq), F32), pltpu.VMEM((2, 1, tq), F32),
                        pltpu.VMEM((2, LANES, tq), F32), pltpu.VMEM((2, 2, tq, tq), BF16)],
        compiler_params=_params("arbitrary", "arbitrary"),
        name="diff_attention",
    )(lam_vecs, subln_g, qt, k, vt)


def _split3(x):
    hi = x.astype(BF16)
    r = x - hi.astype(F32)
    mid = r.astype(BF16)
    lo = (r - mid.astype(F32)).astype(BF16)
    return hi, mid, lo


def _chunk_gates(raw, bias, tri):
    z = raw + bias
    log_f = jnp.minimum(z, 0.0) - jnp.log1p(jnp.exp(-jnp.abs(z)))
    hi, mid, lo = _split3(log_f)
    b = (_dot(hi, tri) + _dot(mid, tri) + _dot(lo, tri))[HEADS:]
    return z[0:HEADS] - b, b


def _mlstm_kernel(q_ref, k_ref, vt_ref, og_ref, gates_ref, bias_ref, normg_ref, o_ref,
                  state_ref, m_ref):
    blk = ML_BLOCK

    @pl.when(pl.program_id(1) == 0)
    def _():
        state_ref[...] = jnp.zeros(state_ref.shape, F32)
        m_ref[...] = jnp.zeros(m_ref.shape, F32)

    src = lax.broadcasted_iota(jnp.int32, (blk, blk), 0)
    dst = lax.broadcasted_iota(jnp.int32, (blk, blk), 1)
    causal = src <= dst
    lane = lax.broadcasted_iota(jnp.int32, (blk, LANES), 1)
    ones_rows = jnp.ones((HEAD_V, blk), BF16)
    tri = jnp.where(causal, 1.0, 0.0).astype(BF16)
    normg = normg_ref[...]
    bias = bias_ref[...]

    for chunk in range(ML_STEP_CHUNKS):
        rows = slice(chunk * blk, (chunk + 1) * blk)
        g_all, b_all = _chunk_gates(gates_ref[:, rows], bias, tri)
        for h in range(HEADS):
            pair = (h // 2) * LANES
            in_head = (lane // ML_QK) == (h % 2)
            q2 = q_ref[rows, pair:pair + LANES]
            k2 = k_ref[rows, pair:pair + LANES]
            kh = jnp.where(in_head, k2, jnp.zeros_like(k2))
            v_ext = jnp.concatenate([vt_ref[h * HEAD_V:(h + 1) * HEAD_V, rows], ones_rows], axis=0)

            b_row = b_all[h:h + 1]
            g_lanes = jnp.broadcast_to(g_all[h:h + 1], (LANES, blk)).T
            g_src = jnp.concatenate([g_lanes] * (blk // LANES), axis=1)
            m_prev = m_ref[h:h + 1, 0:1]

            run_max = jnp.max(jnp.where(causal, g_src, -jnp.inf), axis=0, keepdims=True)
            m_run = jnp.maximum(m_prev, run_max)
            s_t = _dot_nt(kh, q2)
            w_t = (jnp.where(causal, jnp.exp(g_src - m_run), 0.0) * s_t).astype(BF16)
            state = state_ref[h]
            nd = _dot(v_ext, w_t) + jnp.exp(m_prev - m_run) * _dot_nt(state.astype(BF16), q2)
            num = nd[0:HEAD_V]
            den = nd[HEAD_V:]
            hid = num / jnp.maximum(jnp.abs(den), jnp.exp(-(b_row + m_run)))
            hid = hid * lax.rsqrt(jnp.mean(hid * hid, axis=0, keepdims=True) + RMS_EPS)
            out = hid.T * normg * og_ref[rows, h * HEAD_V:(h + 1) * HEAD_V].astype(F32)
            o_ref[rows, h * HEAD_V:(h + 1) * HEAD_V] = out.astype(BF16)

            m_last = m_run[:, blk - 1:blk]
            kw = jnp.where(in_head, k2.astype(F32) * jnp.exp(g_lanes - m_last), 0.0)
            state_ref[h] = jnp.exp(m_prev - m_last) * state + _dot(v_ext, kw.astype(BF16))
            m_ref[h:h + 1, 0:1] = b_row[:, blk - 1:blk] + m_last


def _mlstm_call(q, k, vt, og, gates_t, gate_b, norm_g, layer, batch, seq):
    step = ML_STEP_CHUNKS * ML_BLOCK
    ns = seq // step
    row = lambda w: pl.BlockSpec((step, w), lambda b, c: (b * ns + c, 0))
    return pl.pallas_call(
        _mlstm_kernel,
        grid=(batch, ns),
        in_specs=[row(HEADS * ML_QK), row(HEADS * ML_QK),
                  pl.BlockSpec((GROUP_W, step), lambda b, c: (b, c)), row(GROUP_W),
                  pl.BlockSpec((N_GATES, step), lambda b, c: (b, c)),
                  _resident(gate_b, layer), _resident(norm_g, layer)],
        out_specs=row(GROUP_W),
        out_shape=jax.ShapeDtypeStruct(og.shape, BF16),
        scratch_shapes=[pltpu.VMEM((HEADS, 2 * HEAD_V, LANES), F32),
                        pltpu.VMEM((SUBLANES, LANES), F32)],
        compiler_params=_params("arbitrary", "arbitrary"),
        name="mlstm",
    )(q, k, vt, og, gates_t, gate_b, norm_g)


def _rope_tables(seq):
    inv_freq = ROPE_THETA ** (-jnp.arange(0, DA_HEAD_DIM, 2, dtype=F32) / DA_HEAD_DIM)
    ang = jnp.arange(seq, dtype=F32)[:, None] * inv_freq[None, :]
    cos, sin = jnp.cos(ang), jnp.sin(ang)
    reps = LANES // DA_HEAD_DIM
    cos_l = jnp.tile(jnp.concatenate([cos, cos], axis=1), (1, reps))
    sin_l = jnp.tile(jnp.concatenate([-sin, sin], axis=1), (1, reps))
    return cos_l, sin_l, cos.T, sin.T


def _mixer(x, p, layer, lam_init, rope, batch, seq):
    qt, k, vt, mq, mk, mv, mo, gates_t = _inproj_call(
        x, p["gains"], p["w_all"], p["w_cols_t"], rope, p["conv_w"], p["conv_b"],
        layer, batch, seq)
    y_da = _attn_call(qt, k, vt, p["lam"], p["subln_g"], layer, lam_init, batch, seq)
    y_ml = _mlstm_call(mq, mk, mv, mo, gates_t, p["gate_b"], p["norm_g"], layer, batch, seq)
    return y_da, y_ml


def _prepare(norm_gains, mix_w_in, mix_w_out, da_lambda, da_subln_g, ml_conv_w, ml_conv_b,
             ml_gate_b, ml_norm_g):
    gate_pad = jnp.zeros((DEPTH, D_MODEL, LANES - N_GATES), F32)
    return {
        "gains": norm_gains,
        "w_all": mix_w_in.astype(BF16),
        "w_cols_t": jnp.concatenate([mix_w_in[:, :, 0:GROUP_W],
                                     mix_w_in[:, :, 2 * GROUP_W:3 * GROUP_W],
                                     mix_w_in[:, :, 4 * GROUP_W:5 * GROUP_W],
                                     mix_w_in[:, :, N_MAIN:], gate_pad],
                                    axis=2).transpose(0, 2, 1).astype(BF16),
        "w_mix_out": mix_w_out,
        "lam": da_lambda,
        "subln_g": da_subln_g[:, None, :],
        "conv_w": ml_conv_w,
        "conv_b": ml_conv_b[:, None, :],
        "gate_b": ml_gate_b[:, :, None],
        "norm_g": ml_norm_g[:, None, :],
    }


def kernel(x, ffn_w_in, ffn_w_out, norm_gains, mix_w_in, mix_w_out, da_lambda, da_subln_g,
           ml_conv_w, ml_conv_b, ml_gate_b, ml_norm_g):
    batch, seq, d = x.shape
    assert d == D_MODEL and (batch * seq) % FFN_ROWS == 0 and seq % PROJ_ROWS == 0
    assert seq % (ML_BLOCK * ML_STEP_CHUNKS) == 0
    assert seq % ATT_BLOCK == 0
    rope = _rope_tables(seq)
    p = _prepare(norm_gains, mix_w_in, mix_w_out, da_lambda, da_subln_g, ml_conv_w, ml_conv_b,
                 ml_gate_b, ml_norm_g)
    h = x.reshape(batch * seq, d)
    for l in range(DEPTH):
        h = _ffn_call(h, norm_gains, ffn_w_in, ffn_w_out, l)
        lam_init = 0.8 - 0.6 * math.exp(-0.3 * l)
        y_da, y_ml = _mixer(h, p, l, lam_init, rope, batch, seq)
        h = _ffn_call(h, norm_gains, ffn_w_in, ffn_w_out, l, mix=(y_da, y_ml, p["w_mix_out"]))
    return h.reshape(batch, seq, d)
```

```python
import functools
import math

import jax
import jax.numpy as jnp
from jax import lax
from jax.experimental import pallas as pl
from jax.experimental.pallas import tpu as pltpu

D_MODEL = 1024
DEPTH = 2
CHUNK = 64
ROPE_THETA = 10000.0
RMS_EPS = 1e-6
D_FF = 2816
HEADS = 4
DA_HEAD_DIM = 64
HEAD_V = 128
GROUP_W = HEADS * HEAD_V
ML_QK = 64
ML_CONV = 4
N_MAIN = 6 * GROUP_W
N_GATES = 2 * HEADS
Q_SCALE = DA_HEAD_DIM ** -0.5 * math.log2(math.e)

LANES = 128
SUBLANES = 8
BF16_ROWS = 16
VMEM_LIMIT = 58 * 1024 * 1024

FFN_ROWS = 512
PROJ_ROWS = 1024
FFN_COLS = 256
ROW_SPLIT = 4
ATT_BLOCK = 512
ATT_L_LIMIT = 2.0 ** 64
ML_BLOCK = 256
ML_STEP_CHUNKS = 4

BF16 = jnp.bfloat16
F32 = jnp.float32


def _dot(a, b):
    return jnp.dot(a, b, preferred_element_type=F32)


def _dot_nt(a, b):
    return lax.dot_general(a, b, (((1,), (1,)), ((), ())), preferred_element_type=F32)


def _rms(x, g):
    return x * lax.rsqrt(jnp.mean(x * x, axis=-1, keepdims=True) + RMS_EPS) * g


def _params(*sem):
    return pltpu.CompilerParams(dimension_semantics=sem, vmem_limit_bytes=VMEM_LIMIT)


def _resident(arr, *lead):
    tail = arr.shape[len(lead):]
    index = tuple(lead) + (0,) * len(tail)
    return pl.BlockSpec((None,) * len(lead) + tail, lambda *_: index,
                        pipeline_mode=pl.Buffered(1))


def _ffn_body(x, g_pre, g_post, win_ref, wout_ref):
    xn = _rms(x, g_pre).astype(BF16)
    acc = jnp.zeros(x.shape, F32)
    for c in range(D_FF // FFN_COLS):
        lo = c * FFN_COLS
        gate = _dot(xn, win_ref[:, lo:lo + FFN_COLS].astype(BF16))
        up = _dot(xn, win_ref[:, D_FF + lo:D_FF + lo + FFN_COLS].astype(BF16))
        act = (gate * jax.nn.sigmoid(gate) * up).astype(BF16)
        acc = acc + _dot(act, wout_ref[lo:lo + FFN_COLS, :].astype(BF16))
    return x + 0.5 * _rms(acc, g_post)


def _ffn_kernel(x_ref, gains_ref, win_ref, wout_ref, o_ref):
    o_ref[...] = _ffn_body(x_ref[...], gains_ref[0:1], gains_ref[1:2], win_ref, wout_ref)


def _mix_ffn_kernel(x_ref, yda_ref, yml_ref, wmix_ref, gains_ref, win_ref, wout_ref, o_ref):
    h = (_dot(yda_ref[...], wmix_ref[0:GROUP_W, :].astype(BF16))
         + _dot(yml_ref[...], wmix_ref[GROUP_W:, :].astype(BF16)))
    x = x_ref[...] + _rms(h, gains_ref[3:4])
    o_ref[...] = _ffn_body(x, gains_ref[4:5], gains_ref[5:6], win_ref, wout_ref)


def _ffn_call(x, gains, w_in, w_out, layer, mix=None):
    rows = x.shape[0]
    tm = FFN_ROWS
    which = 0 if mix is None else 1
    row_spec = pl.BlockSpec((tm, D_MODEL), lambda i: (i, 0))
    half_spec = pl.BlockSpec((tm, GROUP_W), lambda i: (i, 0))
    weights = [_resident(gains, layer), _resident(w_in, layer, which),
               _resident(w_out, layer, which)]
    if mix is None:
        kern, ins, specs = _ffn_kernel, (x, gains, w_in, w_out), [row_spec] + weights
    else:
        y_da, y_ml, w_mix = mix
        kern = _mix_ffn_kernel
        ins = (x, y_da, y_ml, w_mix, gains, w_in, w_out)
        specs = [row_spec, half_spec, half_spec, _resident(w_mix, layer)] + weights
    return pl.pallas_call(
        kern,
        grid=(rows // tm,),
        in_specs=specs,
        out_specs=row_spec,
        out_shape=jax.ShapeDtypeStruct(x.shape, F32),
        compiler_params=_params("arbitrary"),
        name="ffn" if mix is None else "mix_ffn",
    )(*ins)


def _rope(t, cos, sin_signed, first_half):
    half = DA_HEAD_DIM // 2
    swapped = jnp.where(first_half, pltpu.roll(t, LANES - half, axis=1), pltpu.roll(t, half, axis=1))
    return t * cos + swapped * sin_signed


def _rope_rows(t, cos, sin):
    half = DA_HEAD_DIM // 2
    a, b = t[0:half], t[half:]
    return jnp.concatenate([a * cos - b * sin, b * cos + a * sin], axis=0)


def _inproj_kernel(tiles_per_seq, x_ref, gain_ref, w_ref, wt_ref, cos_ref, sin_ref,
                   cost_ref, sint_ref, convw_ref, convb_ref,
                   qt_ref, k_ref, vt_ref, mq_ref, mk_ref, mv_ref, mo_ref, gates_ref,
                   conv_ref):
    tm = x_ref.shape[0]
    halo = SUBLANES
    sub = tm // ROW_SPLIT

    @pl.when(pl.program_id(0) % tiles_per_seq == 0)
    def _():
        conv_ref[0:halo, :] = jnp.zeros((halo, GROUP_W), F32)

    lane = lax.broadcasted_iota(jnp.int32, (sub, LANES), 1)
    first_half = (lane % DA_HEAD_DIM) < DA_HEAD_DIM // 2

    for part in range(ROW_SPLIT):
        r0 = part * sub
        rows = slice(r0, r0 + sub)
        xn = _rms(x_ref[rows, :], gain_ref[2:3]).astype(BF16)

        qv_t = _dot_nt(wt_ref[...], xn)
        gates_ref[:, rows] = qv_t[3 * GROUP_W:3 * GROUP_W + N_GATES, :]
        cos_t = cost_ref[:, rows]
        sin_t = sint_ref[:, rows]
        for r in range(GROUP_W // DA_HEAD_DIM):
            lo = r * DA_HEAD_DIM
            qt_ref[lo:lo + DA_HEAD_DIM, rows] = (
                _rope_rows(qv_t[lo:lo + DA_HEAD_DIM], cos_t, sin_t) * Q_SCALE).astype(BF16)
        vt_ref[:, rows] = qv_t[GROUP_W:2 * GROUP_W, :].astype(BF16)
        mv_ref[:, rows] = qv_t[2 * GROUP_W:3 * GROUP_W, :].astype(BF16)

        cos = cos_ref[rows, :]
        sin = sin_ref[rows, :]
        kk = _dot(xn, w_ref[:, GROUP_W:2 * GROUP_W])
        for h in range(HEADS):
            lo = h * LANES
            k_ref[rows, lo:lo + LANES] = _rope(kk[:, lo:lo + LANES], cos, sin, first_half).astype(BF16)

        base = halo + r0
        conv_ref[base:base + sub, :] = _dot(xn, w_ref[:, 3 * GROUP_W:4 * GROUP_W])
        y = convb_ref[...] + convw_ref[ML_CONV - 1:ML_CONV, :] * conv_ref[base:base + sub, :]
        for j in range(ML_CONV - 1):
            back = ML_CONV - 1 - j
            y = y + convw_ref[j:j + 1, :] * conv_ref[base - back:base - back + sub, :]
        y = y * jax.nn.sigmoid(y)
        mq_ref[rows, :] = (y[:, 0:HEADS * ML_QK] * (ML_QK ** -0.5)).astype(BF16)
        mk_ref[rows, :] = y[:, HEADS * ML_QK:].astype(BF16)

        mo_ref[rows, :] = jax.nn.sigmoid(
            _dot(xn, w_ref[:, 5 * GROUP_W:6 * GROUP_W])).astype(BF16)

    conv_ref[0:halo, :] = conv_ref[tm:tm + halo, :]


def _inproj_call(x, gains, w_all, w_cols_t, rope, conv_w, conv_b, layer, batch, seq):
    rows = x.shape[0]
    tm = PROJ_ROWS
    tiles_per_seq = seq // tm
    row = lambda w: pl.BlockSpec((tm, w), lambda i: (i, 0))
    rope_spec = pl.BlockSpec((tm, LANES), lambda i: (i % tiles_per_seq, 0))
    rope_t_spec = pl.BlockSpec((DA_HEAD_DIM // 2, tm), lambda i: (0, i % tiles_per_seq))
    t_spec = pl.BlockSpec((GROUP_W, tm), lambda i: (i // tiles_per_seq, i % tiles_per_seq))
    out_shapes = (
        jax.ShapeDtypeStruct((batch * GROUP_W, seq), BF16),
        jax.ShapeDtypeStruct((rows, GROUP_W), BF16),
        jax.ShapeDtypeStruct((batch * GROUP_W, seq), BF16),
        jax.ShapeDtypeStruct((rows, HEADS * ML_QK), BF16),
        jax.ShapeDtypeStruct((rows, HEADS * ML_QK), BF16),
        jax.ShapeDtypeStruct((batch * GROUP_W, seq), BF16),
        jax.ShapeDtypeStruct((rows, GROUP_W), BF16),
        jax.ShapeDtypeStruct((batch * N_GATES, seq), F32),
    )
    gate_spec = pl.BlockSpec((N_GATES, tm), lambda i: (i // tiles_per_seq, i % tiles_per_seq))
    wt_spec = pl.BlockSpec((None, 3 * GROUP_W + BF16_ROWS, D_MODEL), lambda i: (layer, 0, 0),
                           pipeline_mode=pl.Buffered(1))
    out_specs = (t_spec, row(GROUP_W), t_spec, row(HEADS * ML_QK), row(HEADS * ML_QK),
                 t_spec, row(GROUP_W), gate_spec)
    cos, sin, cos_t, sin_t = rope
    return pl.pallas_call(
        functools.partial(_inproj_kernel, tiles_per_seq),
        grid=(rows // tm,),
        in_specs=[row(D_MODEL), _resident(gains, layer), _resident(w_all, layer),
                  wt_spec,
                  rope_spec, rope_spec, rope_t_spec, rope_t_spec,
                  _resident(conv_w, layer), _resident(conv_b, layer)],
        out_specs=out_specs,
        out_shape=out_shapes,
        scratch_shapes=[pltpu.VMEM((tm + 2 * SUBLANES, GROUP_W), F32)],
        compiler_params=_params("arbitrary"),
        name="mixer_inproj",
    )(x, gains, w_all, w_cols_t, cos, sin, cos_t, sin_t, conv_w, conv_b)


def _attn_kernel(lam_init, lam_ref, subg_ref, qt_ref, k_ref, vt_ref, o_ref,
                 m_sc, l_sc, acc_sc, s_sc):
    blk = ATT_BLOCK
    n_blocks = qt_ref.shape[1] // blk

    lv = lam_ref[...]
    lam = (jnp.exp(jnp.sum(lv[0:1] * lv[1:2], axis=-1, keepdims=True))
           - jnp.exp(jnp.sum(lv[2:3] * lv[3:4], axis=-1, keepdims=True)) + lam_init)

    def write_out(block):
        o_t = acc_sc[0] / l_sc[0] - lam * (acc_sc[1] / l_sc[1])
        start = pl.multiple_of(block * blk, blk)
        o_ref[pl.ds(start, blk), :] = (
            _rms(o_t.T, subg_ref[...]) * (1.0 - lam_init)).astype(BF16)

    l_sc[...] = jnp.ones(l_sc.shape, F32)
    acc_sc[...] = jnp.zeros(acc_sc.shape, F32)

    def query_block(i, carry):
        q_start = pl.multiple_of(i * blk, blk)
        qt = qt_ref[:, pl.ds(q_start, blk)]
        feat = lax.broadcasted_iota(jnp.int32, qt.shape, 0)
        zero = jnp.zeros_like(qt)
        qt_halves = (jnp.where(feat < DA_HEAD_DIM, qt, zero), jnp.where(feat >= DA_HEAD_DIM, qt, zero))

        def chunk_mask():
            key = lax.broadcasted_iota(jnp.int32, (blk, blk), 0) // CHUNK
            qry = lax.broadcasted_iota(jnp.int32, (blk, blk), 1) // CHUNK
            return key <= qry

        def key_block(j):
            start = pl.multiple_of(j * blk, blk)
            return k_ref[pl.ds(start, blk), :], vt_ref[:, pl.ds(start, blk)]

        def online_step(j, mask):
            kb, vb = key_block(j)
            for c in range(2):
                s = _dot(kb, qt_halves[c])
                if mask is not None:
                    s = jnp.where(mask, s, -jnp.inf)
                m_old = m_sc[c]
                m_new = jnp.maximum(m_old, jnp.max(s, axis=0, keepdims=True))
                alpha = jnp.exp2(m_old - m_new)
                p = jnp.exp2(s - m_new)
                l_sc[c] = alpha * l_sc[c] + jnp.sum(p, axis=0, keepdims=True)
                acc_sc[c] = alpha * acc_sc[c] + _dot(vb, p.astype(BF16))
                m_sc[c] = m_new

        def scores(j, slot, mask=None):
            kb, _ = key_block(j)
            for c in range(2):
                s = _dot(kb, qt_halves[c])
                if mask is not None:
                    s = jnp.where(mask, s, -jnp.inf)
                    m_sc[c] = jnp.max(s, axis=0, keepdims=True)
                p = jnp.exp2(s - m_sc[c])
                l_sc[c] += jnp.sum(p, axis=0, keepdims=True)
                s_sc[slot, c] = p.astype(BF16)

        def accumulate(j, slot):
            _, vb = key_block(j)
            for c in range(2):
                acc_sc[c] += _dot(vb, s_sc[slot, c])

        write_out(jnp.maximum(i - 1, 0))
        l_sc[...] = jnp.zeros(l_sc.shape, F32)
        acc_sc[...] = jnp.zeros(acc_sc.shape, F32)
        scores(i, 0, chunk_mask())

        def pair(j):
            scores(j, 1)
            accumulate(jnp.where(j == 0, i, j - 1), 0)
            scores(j + 1, 0)
            accumulate(j, 1)

        def octet(t, carry):
            for u in range(4):
                pair(8 * t + 2 * u)
            return carry

        lax.fori_loop(0, i // 8, octet, 0)
        left = i % 8

        @pl.when(left >= 4)
        def _():
            pair(i - left)
            pair(i - left + 2)

        @pl.when(left % 4 >= 2)
        def _():
            pair(i - left % 4)

        @pl.when(i % 2 == 1)
        def _():
            scores(i - 1, 1)
            accumulate(jnp.where(i == 1, i, i - 2), 0)
            accumulate(i - 1, 1)

        @pl.when(i % 2 == 0)
        def _():
            accumulate(jnp.where(i == 0, i, i - 1), 0)

        in_range = jnp.all(l_sc[...] < ATT_L_LIMIT)

        @pl.when(jnp.logical_not(in_range))
        def _():
            m_sc[...] = jnp.full(m_sc.shape, -jnp.inf, F32)
            l_sc[...] = jnp.zeros(l_sc.shape, F32)
            acc_sc[...] = jnp.zeros(acc_sc.shape, F32)
            online_step(i, chunk_mask())

            def body(j, carry):
                online_step(j, None)
                return carry

            lax.fori_loop(0, i, body, 0)

        return carry

    lax.fori_loop(0, n_blocks, query_block, 0)
    write_out(n_blocks - 1)


def _attn_call(qt, k, vt, lam_vecs, subln_g, layer, lam_init, batch, seq):
    tq = ATT_BLOCK
    t_spec = pl.BlockSpec((LANES, seq), lambda b, h: (b * HEADS + h, 0))
    row_spec = pl.BlockSpec((seq, LANES), lambda b, h: (b, h))
    return pl.pallas_call(
        functools.partial(_attn_kernel, lam_init),
        grid=(batch, HEADS),
        in_specs=[_resident(lam_vecs, layer), _resident(subln_g, layer), t_spec, row_spec, t_spec],
        out_specs=row_spec,
        out_shape=jax.ShapeDtypeStruct(k.shape, BF16),
        scratch_shapes=[pltpu.VMEM((2, 1, tq), F32), pltpu.VMEM((2, 1, tq), F32),
                        pltpu.VMEM((2, LANES, tq), F32), pltpu.VMEM((2, 2, tq, tq), BF16)],
        compiler_params=_params("arbitrary", "arbitrary"),
        name="diff_attention",
    )(lam_vecs, subln_g, qt, k, vt)


def _split3(x):
    hi = x.astype(BF16)
    r = x - hi.astype(F32)
    mid = r.astype(BF16)
    lo = (r - mid.astype(F32)).astype(BF16)
    return hi, mid, lo


def _chunk_gates(raw, bias, tri):
    z = raw + bias
    log_f = jnp.minimum(z, 0.0) - jnp.log1p(jnp.exp(-jnp.abs(z)))
    hi, mid, lo = _split3(log_f)
    b = (_dot(hi, tri) + _dot(mid, tri) + _dot(lo, tri))[HEADS:]
    return z[0:HEADS] - b, b


def _mlstm_kernel(q_ref, k_ref, vt_ref, og_ref, gates_ref, bias_ref, normg_ref, o_ref,
                  state_ref, m_ref):
    blk = ML_BLOCK

    @pl.when(pl.program_id(1) == 0)
    def _():
        state_ref[...] = jnp.zeros(state_ref.shape, F32)
        m_ref[...] = jnp.zeros(m_ref.shape, F32)

    src = lax.broadcasted_iota(jnp.int32, (blk, blk), 0)
    dst = lax.broadcasted_iota(jnp.int32, (blk, blk), 1)
    causal = src <= dst
    lane = lax.broadcasted_iota(jnp.int32, (blk, LANES), 1)
    ones_rows = jnp.ones((HEAD_V, blk), BF16)
    tri = jnp.where(causal, 1.0, 0.0).astype(BF16)
    normg = normg_ref[...]
    bias = bias_ref[...]

    for chunk in range(ML_STEP_CHUNKS):
        rows = slice(chunk * blk, (chunk + 1) * blk)
        g_all, b_all = _chunk_gates(gates_ref[:, rows], bias, tri)
        for h in range(HEADS):
            pair = (h // 2) * LANES
            in_head = (lane // ML_QK) == (h % 2)
            q2 = q_ref[rows, pair:pair + LANES]
            k2 = k_ref[rows, pair:pair + LANES]
            kh = jnp.where(in_head, k2, jnp.zeros_like(k2))
            v_ext = jnp.concatenate([vt_ref[h * HEAD_V:(h + 1) * HEAD_V, rows], ones_rows], axis=0)

            b_row = b_all[h:h + 1]
            g_lanes = jnp.broadcast_to(g_all[h:h + 1], (LANES, blk)).T
            g_src = jnp.concatenate([g_lanes] * (blk // LANES), axis=1)
            m_prev = m_ref[h:h + 1, 0:1]

            run_max = jnp.max(jnp.where(causal, g_src, -jnp.inf), axis=0, keepdims=True)
            m_run = jnp.maximum(m_prev, run_max)
            s_t = _dot_nt(kh, q2)
            w_t = (jnp.where(causal, jnp.exp(g_src - m_run), 0.0) * s_t).astype(BF16)
            state = state_ref[h]
            nd = _dot(v_ext, w_t) + jnp.exp(m_prev - m_run) * _dot_nt(state.astype(BF16), q2)
            num = nd[0:HEAD_V]
            den = nd[HEAD_V:]
            hid = num / jnp.maximum(jnp.abs(den), jnp.exp(-(b_row + m_run)))
            hid = hid * lax.rsqrt(jnp.mean(hid * hid, axis=0, keepdims=True) + RMS_EPS)
            out = hid.T * normg * og_ref[rows, h * HEAD_V:(h + 1) * HEAD_V].astype(F32)
            o_ref[rows, h * HEAD_V:(h + 1) * HEAD_V] = out.astype(BF16)

            m_last = m_run[:, blk - 1:blk]
            kw = jnp.where(in_head, k2.astype(F32) * jnp.exp(g_lanes - m_last), 0.0)
            state_ref[h] = jnp.exp(m_prev - m_last) * state + _dot(v_ext, kw.astype(BF16))
            m_ref[h:h + 1, 0:1] = b_row[:, blk - 1:blk] + m_last


def _mlstm_call(q, k, vt, og, gates_t, gate_b, norm_g, layer, batch, seq):
    step = ML_STEP_CHUNKS * ML_BLOCK
    ns = seq // step
    row = lambda w: pl.BlockSpec((step, w), lambda b, c: (b * ns + c, 0))
    return pl.pallas_call(
        _mlstm_kernel,
        grid=(batch, ns),
        in_specs=[row(HEADS * ML_QK), row(HEADS * ML_QK),
                  pl.BlockSpec((GROUP_W, step), lambda b, c: (b, c)), row(GROUP_W),
                  pl.BlockSpec((N_GATES, step), lambda b, c: (b, c)),
                  _resident(gate_b, layer), _resident(norm_g, layer)],
        out_specs=row(GROUP_W),
        out_shape=jax.ShapeDtypeStruct(og.shape, BF16),
        scratch_shapes=[pltpu.VMEM((HEADS, 2 * HEAD_V, LANES), F32),
                        pltpu.VMEM((SUBLANES, LANES), F32)],
        compiler_params=_params("arbitrary", "arbitrary"),
        name="mlstm",
    )(q, k, vt, og, gates_t, gate_b, norm_g)


def _rope_tables(seq):
    inv_freq = ROPE_THETA ** (-jnp.arange(0, DA_HEAD_DIM, 2, dtype=F32) / DA_HEAD_DIM)
    ang = jnp.arange(seq, dtype=F32)[:, None] * inv_freq[None, :]
    cos, sin = jnp.cos(ang), jnp.sin(ang)
    reps = LANES // DA_HEAD_DIM
    cos_l = jnp.tile(jnp.concatenate([cos, cos], axis=1), (1, reps))
    sin_l = jnp.tile(jnp.concatenate([-sin, sin], axis=1), (1, reps))
    return cos_l, sin_l, cos.T, sin.T


def _mixer(x, p, layer, lam_init, rope, batch, seq):
    qt, k, vt, mq, mk, mv, mo, gates_t = _inproj_call(
        x, p["gains"], p["w_all"], p["w_cols_t"], rope, p["conv_w"], p["conv_b"],
        layer, batch, seq)
    y_da = _attn_call(qt, k, vt, p["lam"], p["subln_g"], layer, lam_init, batch, seq)
    y_ml = _mlstm_call(mq, mk, mv, mo, gates_t, p["gate_b"], p["norm_g"], layer, batch, seq)
    return y_da, y_ml


def _prepare(norm_gains, mix_w_in, mix_w_out, da_lambda, da_subln_g, ml_conv_w, ml_conv_b,
             ml_gate_b, ml_norm_g):
    gate_pad = jnp.zeros((DEPTH, D_MODEL, LANES - N_GATES), F32)
    return {
        "gains": norm_gains,
        "w_all": mix_w_in.astype(BF16),
        "w_cols_t": jnp.concatenate([mix_w_in[:, :, 0:GROUP_W],
                                     mix_w_in[:, :, 2 * GROUP_W:3 * GROUP_W],
                                     mix_w_in[:, :, 4 * GROUP_W:5 * GROUP_W],
                                     mix_w_in[:, :, N_MAIN:], gate_pad],
                                    axis=2).transpose(0, 2, 1).astype(BF16),
        "w_mix_out": mix_w_out,
        "lam": da_lambda,
        "subln_g": da_subln_g[:, None, :],
        "conv_w": ml_conv_w,
        "conv_b": ml_conv_b[:, None, :],
        "gate_b": ml_gate_b[:, :, None],
        "norm_g": ml_norm_g[:, None, :],
    }


def kernel(x, ffn_w_in, ffn_w_out, norm_gains, mix_w_in, mix_w_out, da_lambda, da_subln_g,
           ml_conv_w, ml_conv_b, ml_gate_b, ml_norm_g):
    batch, seq, d = x.shape
    assert d == D_MODEL and (batch * seq) % FFN_ROWS == 0 and seq % PROJ_ROWS == 0
    assert seq % (ML_BLOCK * ML_STEP_CHUNKS) == 0
    assert seq % ATT_BLOCK == 0
    rope = _rope_tables(seq)
    p = _prepare(norm_gains, mix_w_in, mix_w_out, da_lambda, da_subln_g, ml_conv_w, ml_conv_b,
                 ml_gate_b, ml_norm_g)
    h = x.reshape(batch * seq, d)
    for l in range(DEPTH):
        h = _ffn_call(h, norm_gains, ffn_w_in, ffn_w_out, l)
        lam_init = 0.8 - 0.6 * math.exp(-0.3 * l)
        y_da, y_ml = _mixer(h, p, l, lam_init, rope, batch, seq)
        h = _ffn_call(h, norm_gains, ffn_w_in, ffn_w_out, l, mix=(y_da, y_ml, p["w_mix_out"]))
    return h.reshape(batch, seq, d)
```
